```python
import math
import jax, jax.numpy as jnp
from jax import lax
import numpy as np

D_MODEL = 1024
BATCH = 16
SEQ = 2048
DEPTH = 1

CHUNK = 64
Q_BLOCK = 128

MLA_HEADS = 8
QK_NOPE_DIM = 64
QK_ROPE_DIM = 32
V_HEAD_DIM = 64
Q_LORA_RANK = 256
KV_LORA_RANK = 128
ROPE_THETA = 10000.0
ATTN_WIDTH = MLA_HEADS * V_HEAD_DIM

CONV_GROUPS = 8
CONV_DIM = D_MODEL // 2
CONV_WIDTH = 3

MIX_WIDTH = ATTN_WIDTH + CONV_DIM
IN_SIZES = (Q_LORA_RANK, KV_LORA_RANK, QK_ROPE_DIM, CONV_DIM, CONV_DIM, CONV_DIM)
IN_COLS = sum(IN_SIZES)
IN_OFFSETS = tuple(int(v) for v in np.cumsum(IN_SIZES)[:-1])

N_GROUPS = 4
EXPERTS_PER_GROUP = 8
N_EXPERTS = N_GROUPS * EXPERTS_PER_GROUP
TOP_K = 2
D_EXPERT = 256
DISPATCH_BLOCK = 128

N_MOD = 6
RMS_EPS = 1e-6
NEG_INF = -1e30

kernel_name = 'hymba_mla_shortconv_hmoe_adaln_block'


def rms_norm(x, g):
    xf = x.astype(jnp.float32)
    y = xf * lax.rsqrt(jnp.mean(xf * xf, axis=-1, keepdims=True) + RMS_EPS)
    return (y * g.astype(jnp.float32)).astype(x.dtype)


def modulate(h, shift, scale):
    return h * (1.0 + scale[:, None, :]) + shift[:, None, :]


def rope_tables(positions):
    inv_freq = 1.0 / (ROPE_THETA ** (jnp.arange(0, QK_ROPE_DIM, 2, dtype=jnp.float32) / QK_ROPE_DIM))
    ang = positions.astype(jnp.float32)[..., None] * inv_freq
    return jnp.cos(ang), jnp.sin(ang)


def apply_rope(x, cos, sin):
    xf = x.astype(jnp.float32)
    x1, x2 = xf[..., : QK_ROPE_DIM // 2], xf[..., QK_ROPE_DIM // 2:]
    return jnp.concatenate([x1 * cos - x2 * sin, x2 * cos + x1 * sin], axis=-1).astype(x.dtype)


def mla_mixer(q_c, kv_c, k_r, positions, q_norm_g, kv_norm_g, w_uq, w_uk, w_uv):
    B, S, _ = q_c.shape
    H, DN, DR, DV = MLA_HEADS, QK_NOPE_DIM, QK_ROPE_DIM, V_HEAD_DIM
    q = (rms_norm(q_c, q_norm_g) @ w_uq).reshape(B, S, H, DN + DR)
    q_nope, q_rope = q[..., :DN], q[..., DN:]
    kv = rms_norm(kv_c, kv_norm_g)
    k_nope = (kv @ w_uk).reshape(B, S, H, DN)
    v = (kv @ w_uv).reshape(B, S, H, DV)
    cos, sin = rope_tables(positions)
    q_rope = apply_rope(q_rope, cos[:, :, None, :], sin[:, :, None, :])
    k_rope = apply_rope(k_r, cos, sin)
    nq = S // Q_BLOCK
    qn_b = q_nope.reshape(B, nq, Q_BLOCK, H, DN).transpose(1, 0, 2, 3, 4)
    qr_b = q_rope.reshape(B, nq, Q_BLOCK, H, DR).transpose(1, 0, 2, 3, 4)
    k_chunk = jnp.arange(S) // CHUNK
    sm_scale = 1.0 / math.sqrt(DN + DR)

    def attend_block(args):
        qn, qr, bi = args
        s = (jnp.einsum('bqhd,bkhd->bhqk', qn, k_nope)
             + jnp.einsum('bqhr,bkr->bhqk', qr, k_rope)).astype(jnp.float32) * sm_scale
        q_chunk = (bi * Q_BLOCK + jnp.arange(Q_BLOCK)) // CHUNK
        allowed = k_chunk[None, :] <= q_chunk[:, None]
        s = jnp.where(allowed[None, None], s, NEG_INF)
        p = jax.nn.softmax(s, axis=-1).astype(v.dtype)
        return jnp.einsum('bhqk,bkhd->bqhd', p, v)

    out = lax.map(attend_block, (qn_b, qr_b, jnp.arange(nq)))
    return out.transpose(1, 0, 2, 3, 4).reshape(B, S, ATTN_WIDTH)


def short_conv_mixer(u, b_gate, c_gate, conv_w, conv_b):
    z = c_gate * u
    rhs = conv_w.reshape(CONV_WIDTH, 1, CONV_DIM).astype(z.dtype)
    zc = lax.conv_general_dilated(z, rhs, window_strides=(1,), padding=[(CONV_WIDTH - 1, 0)],
                                  dimension_numbers=('NWC', 'WIO', 'NWC'),
                                  feature_group_count=CONV_DIM)
    return b_gate * (zc + conv_b)


def hier_moe(h, w_rg, b_rg, w_re, b_re, w_gate, w_up, w_down):
    B, S, D = h.shape
    T = B * S
    hf = h.reshape(T, D)
    g_logits = (hf @ w_rg + b_rg).astype(jnp.float32)
    g_prob = jax.nn.softmax(g_logits, axis=-1)
    g_idx = jnp.argmax(g_logits, axis=-1)
    p_group = jnp.take_along_axis(g_prob, g_idx[:, None], axis=1)[:, 0]
    e_logits = (hf @ w_re + b_re).astype(jnp.float32).reshape(T, N_GROUPS, EXPERTS_PER_GROUP)
    e_sel = jnp.take_along_axis(e_logits, g_idx[:, None, None], axis=1)[:, 0]
    top_v, top_i = lax.top_k(e_sel, TOP_K)
    gates = jax.nn.softmax(top_v, axis=-1) * p_group[:, None]
    expert = g_idx[:, None] * EXPERTS_PER_GROUP + top_i
    A = T * TOP_K
    flat_e = expert.reshape(A)
    flat_tok = jnp.repeat(jnp.arange(T, dtype=jnp.int32), TOP_K)
    flat_w = gates.reshape(A)
    order = jnp.argsort(flat_e)
    se, stok, sw = flat_e[order], flat_tok[order], flat_w[order]
    counts = jnp.bincount(flat_e, length=N_EXPERTS)
    starts = jnp.cumsum(counts) - counts
    padded = (counts + DISPATCH_BLOCK - 1) // DISPATCH_BLOCK * DISPATCH_BLOCK
    pend = jnp.cumsum(padded)
    pstarts = pend - padded
    dest = pstarts[se] + (jnp.arange(A) - starts[se])
    n_blocks = -(-A // DISPATCH_BLOCK) + N_EXPERTS
    R = n_blocks * DISPATCH_BLOCK
    buf_tok = jnp.full((R,), T, dtype=jnp.int32).at[dest].set(stok)
    buf_w = jnp.zeros((R,), jnp.float32).at[dest].set(sw)
    blk_e = jnp.minimum(jnp.searchsorted(pend, jnp.arange(n_blocks) * DISPATCH_BLOCK, side='right'),
                        N_EXPERTS - 1)
    h_pad = jnp.concatenate([hf, jnp.zeros((1, D), hf.dtype)], axis=0)

    def expert_block(args):
        tok, e = args
        xb = h_pad[tok]
        a = jax.nn.silu(xb @ w_gate[e]) * (xb @ w_up[e])
        return a @ w_down[e]

    out = lax.map(expert_block, (buf_tok.reshape(n_blocks, DISPATCH_BLOCK), blk_e))
    out = out.reshape(R, D) * buf_w[:, None].astype(out.dtype)
    y = jnp.zeros((T + 1, D), out.dtype).at[buf_tok].add(out)
    return y[:T].reshape(B, S, D)


def setup_inputs(seed: int = 0) -> dict:
    key = jax.random.key(seed)
    ks = jax.random.split(key, 28)
    L, D = DEPTH, D_MODEL
    f32 = jnp.float32

    def nrm(k, shape, scale):
        return jax.random.normal(k, shape, f32) * scale

    def gain(k, shape):
        return 1.0 + 0.02 * jax.random.normal(k, shape, f32)

    offsets = jax.random.randint(ks[2], (BATCH, 1), 0, 4096, dtype=jnp.int32)
    positions = offsets + jnp.arange(SEQ, dtype=jnp.int32)[None, :]
    return {
        'x': jax.random.normal(ks[0], (BATCH, SEQ, D), f32),
        'c': jax.random.normal(ks[1], (BATCH, D), f32),
        'positions': positions,
        'w_ada': nrm(ks[3], (L, D, N_MOD * D), 0.3 * D ** -0.5),
        'b_ada': nrm(ks[4], (L, N_MOD * D), 0.02),
        'norm1_g': gain(ks[5], (L, D)),
        'w_in': nrm(ks[6], (L, D, IN_COLS), D ** -0.5),
        'q_norm_g': gain(ks[7], (L, Q_LORA_RANK)),
        'kv_norm_g': gain(ks[8], (L, KV_LORA_RANK)),
        'w_uq': nrm(ks[9], (L, Q_LORA_RANK, MLA_HEADS * (QK_NOPE_DIM + QK_ROPE_DIM)), Q_LORA_RANK ** -0.5),
        'w_uk': nrm(ks[10], (L, KV_LORA_RANK, MLA_HEADS * QK_NOPE_DIM), KV_LORA_RANK ** -0.5),
        'w_uv': nrm(ks[11], (L, KV_LORA_RANK, MLA_HEADS * V_HEAD_DIM), KV_LORA_RANK ** -0.5),
        'conv_w': nrm(ks[12], (L, CONV_WIDTH, CONV_DIM), CONV_WIDTH ** -0.5),
        'conv_b': nrm(ks[13], (L, CONV_DIM), 0.02),
        'attn_out_g': gain(ks[14], (L, ATTN_WIDTH)),
        'conv_out_g': gain(ks[15], (L, CONV_DIM)),
        'w_out': nrm(ks[16], (L, MIX_WIDTH, D), MIX_WIDTH ** -0.5),
        'norm2_g': gain(ks[17], (L, D)),
        'w_router_group': nrm(ks[18], (L, D, N_GROUPS), D ** -0.5),
        'b_router_group': nrm(ks[19], (L, N_GROUPS), 0.01),
        'w_router_expert': nrm(ks[20], (L, D, N_EXPERTS), D ** -0.5),
        'b_router_expert': nrm(ks[21], (L, N_EXPERTS), 0.01),
        'w_exp_gate': nrm(ks[22], (L, N_EXPERTS, D, D_EXPERT), D ** -0.5),
        'w_exp_up': nrm(ks[23], (L, N_EXPERTS, D, D_EXPERT), D ** -0.5),
        'w_exp_down': nrm(ks[24], (L, N_EXPERTS, D_EXPERT, D), D_EXPERT ** -0.5),
        'final_g': gain(ks[25], (D,)),
    }


def reference(x, c, positions, w_ada, b_ada, norm1_g, w_in, q_norm_g, kv_norm_g, w_uq, w_uk, w_uv,
              conv_w, conv_b, attn_out_g, conv_out_g, w_out, norm2_g, w_router_group, b_router_group,
              w_router_expert, b_router_expert, w_exp_gate, w_exp_up, w_exp_down, final_g):
    c_act = jax.nn.silu(c)
    for l in range(DEPTH):
        mod = c_act @ w_ada[l] + b_ada[l]
        sh1, sc1, g1, sh2, sc2, g2 = jnp.split(mod, N_MOD, axis=-1)
        h = modulate(rms_norm(x, norm1_g[l]), sh1, sc1)
        proj = h @ w_in[l]
        q_c, kv_c, k_r, u, b_gate, c_gate = jnp.split(proj, IN_OFFSETS, axis=-1)
        attn = mla_mixer(q_c, kv_c, k_r, positions, q_norm_g[l], kv_norm_g[l], w_uq[l], w_uk[l], w_uv[l])
        conv = short_conv_mixer(u, b_gate, c_gate, conv_w[l], conv_b[l])
        mixed = jnp.concatenate([rms_norm(attn, attn_out_g[l]), rms_norm(conv, conv_out_g[l])], axis=-1)
        x = x + g1[:, None, :] * (mixed @ w_out[l])
        h2 = modulate(rms_norm(x, norm2_g[l]), sh2, sc2)
        x = x + g2[:, None, :] * hier_moe(h2, w_router_group[l], b_router_group[l], w_router_expert[l],
                                          b_router_expert[l], w_exp_gate[l], w_exp_up[l], w_exp_down[l])
    return rms_norm(x, final_g)
```

```python
import functools
import math

import jax
import jax.numpy as jnp
from jax import lax
from jax.experimental import pallas as pl
from jax.experimental.pallas import tpu as pltpu

F32 = jnp.float32
BF16 = jnp.bfloat16

CHUNK = 64
HEADS = 8
D_NOPE = 64
D_ROPE = 32
D_V = 64
ROPE_THETA = 10000.0
CONV_WIDTH = 3
N_GROUPS = 4
EXPERTS_PER_GROUP = 8
N_EXPERTS = N_GROUPS * EXPERTS_PER_GROUP
TOP_K = 2
N_MOD = 6
RMS_EPS = 1e-6
NEG_INF = -1e30

LANES = 128
SUBLANES = 8
HEAD_PAD = LANES
VMEM_LIMIT = 56 * 1024 * 1024

TM_PROJ = 256
TQ = 256
TK = 256
TM_OUT = 256
TM_DEST = 512
TM_DISP = 512
BLK = 256
TM_COMB = 256
TN_ADA = 512


def _rms(x, g):
    return x * lax.rsqrt(jnp.mean(x * x, axis=-1, keepdims=True) + RMS_EPS) * g


def _dot(a, b):
    return jnp.dot(a, b, preferred_element_type=F32)


def _dot_nt(a, b):
    return lax.dot_general(a, b, (((1,), (1,)), ((), ())), preferred_element_type=F32)


def _split_bf16(x):
    hi = x.astype(BF16)
    lo = (x - hi.astype(F32)).astype(BF16)
    return hi, lo


def _adaln_kernel(c_ref, w_ref, b_ref, o_ref):
    c = c_ref[...]
    ca = c * jax.nn.sigmoid(c)
    c_hi, c_lo = _split_bf16(ca)
    w_hi, w_lo = _split_bf16(w_ref[...])
    acc = _dot(c_hi, w_hi) + _dot(c_hi, w_lo) + _dot(c_lo, w_hi)
    o_ref[...] = acc + b_ref[...]


def _adaln(c, w, b):
    bsz, d = c.shape
    n = w.shape[1]
    return pl.pallas_call(
        _adaln_kernel,
        out_shape=jax.ShapeDtypeStruct((bsz, n), F32),
        grid=(n // TN_ADA,),
        in_specs=[
            pl.BlockSpec((bsz, d), lambda j: (0, 0)),
            pl.BlockSpec((d, TN_ADA), lambda j: (0, j)),
            pl.BlockSpec((1, TN_ADA), lambda j: (0, j)),
        ],
        out_specs=pl.BlockSpec((bsz, TN_ADA), lambda j: (0, j)),
        compiler_params=pltpu.CompilerParams(
            dimension_semantics=("arbitrary",), vmem_limit_bytes=VMEM_LIMIT),
        name="adaln",
    )(c, w, b.reshape(1, n))


def _inproj_kernel(x_ref, mod_ref, pos_ref, g1_ref, win_ref, gq_ref, gkv_ref, wq1_ref, wq2_ref,
                   wk_ref, wvt_ref, invf_ref, cw_ref, cb_ref, gc_ref,
                   q_out, k_out, vt_out, conv_out, zbuf, *, tm, q_rank, kv_rank, conv_dim):
    si = pl.program_id(1)
    x = x_ref[0]
    mod = mod_ref[0]
    h = (_rms(x, g1_ref[...]) * (1.0 + mod[1:2]) + mod[0:1]).astype(BF16)
    proj = _dot(h, win_ref[...])

    o_kv = q_rank
    o_kr = o_kv + kv_rank
    o_u = o_kr + 2 * LANES
    q_c = proj[:, 0:q_rank]
    kv_c = proj[:, o_kv:o_kv + kv_rank]
    kr = proj[:, o_kr:o_kr + LANES]
    kr_rot = proj[:, o_kr + LANES:o_kr + 2 * LANES]
    u = proj[:, o_u:o_u + conv_dim]
    b_gate = proj[:, o_u + conv_dim:o_u + 2 * conv_dim]
    c_gate = proj[:, o_u + 2 * conv_dim:o_u + 3 * conv_dim]

    ang = pos_ref[0].astype(F32) * invf_ref[...]
    cos_t = jnp.cos(ang)
    sin_t = jnp.sin(ang)
    lane = lax.broadcasted_iota(jnp.int32, (1, LANES), 1)
    is_rope = (lane >= D_NOPE) & (lane < D_NOPE + D_ROPE)
    sm_scale = 1.0 / math.sqrt(D_NOPE + D_ROPE)
    q_cos = jnp.where(lane < D_NOPE, sm_scale, jnp.where(is_rope, cos_t * sm_scale, 0.0))
    q_sin = jnp.where(is_rope, sin_t * sm_scale, 0.0)

    qn = _rms(q_c, gq_ref[...]).astype(BF16)
    qa = _dot(qn, wq1_ref[...])
    qb = _dot(qn, wq2_ref[...])
    kvn = _rms(kv_c, gkv_ref[...]).astype(BF16)
    kn = _dot(kvn, wk_ref[...])
    k_rope = kr * cos_t + kr_rot * sin_t
    for hd in range(HEADS):
        sl = slice(hd * HEAD_PAD, (hd + 1) * HEAD_PAD)
        q_out[0, hd] = (qa[:, sl] * q_cos + qb[:, sl] * q_sin).astype(BF16)
        k_out[0, hd] = (kn[:, sl] + k_rope).astype(BF16)

    vt = _dot_nt(wvt_ref[...], kvn)
    for j in range(tm // TK):
        vt_out[0, j] = vt[:, j * TK:(j + 1) * TK].astype(BF16)

    z = c_gate * u

    @pl.when(si == 0)
    def _():
        zbuf[0:SUBLANES, :] = jnp.zeros((SUBLANES, conv_dim), F32)

    zbuf[SUBLANES:SUBLANES + tm, :] = z
    z1 = zbuf[SUBLANES - 1:SUBLANES - 1 + tm, :]
    z2 = zbuf[SUBLANES - 2:SUBLANES - 2 + tm, :]
    cw = cw_ref[...]
    zc = cw[0:1] * z2 + cw[1:2] * z1 + cw[2:3] * z + cb_ref[...]
    conv = b_gate * zc
    conv_out[0] = _rms(conv, gc_ref[...]).astype(BF16)
    zbuf[0:SUBLANES, :] = zbuf[tm:tm + SUBLANES, :]


def _inproj(x, mod, pos, g1, win, gq, gkv, wq1, wq2, wk, wvt, invf, cw, cb, gc):
    bsz, seq, d = x.shape
    tm = TM_PROJ
    q_rank, kv_rank = wq1.shape[0], wk.shape[0]
    conv_dim = cw.shape[1]
    hv = wvt.shape[0]
    n_s = seq // tm
    const2 = lambda b, s: (0, 0)
    kern = functools.partial(_inproj_kernel, tm=tm, q_rank=q_rank, kv_rank=kv_rank,
                             conv_dim=conv_dim)
    return pl.pallas_call(
        kern,
        out_shape=(
            jax.ShapeDtypeStruct((bsz, HEADS, seq, HEAD_PAD), BF16),
            jax.ShapeDtypeStruct((bsz, HEADS, seq, HEAD_PAD), BF16),
            jax.ShapeDtypeStruct((bsz, seq // TK, hv, TK), BF16),
            jax.ShapeDtypeStruct((bsz, seq, conv_dim), BF16),
        ),
        grid=(bsz, n_s),
        in_specs=[
            pl.BlockSpec((1, tm, d), lambda b, s: (b, s, 0)),
            pl.BlockSpec((1, N_MOD, d), lambda b, s: (b, 0, 0)),
            pl.BlockSpec((1, tm, 1), lambda b, s: (b, s, 0)),
            pl.BlockSpec((1, d), const2),
            pl.BlockSpec(win.shape, const2),
            pl.BlockSpec((1, q_rank), const2),
            pl.BlockSpec((1, kv_rank), const2),
            pl.BlockSpec(wq1.shape, const2),
            pl.BlockSpec(wq2.shape, const2),
            pl.BlockSpec(wk.shape, const2),
            pl.BlockSpec(wvt.shape, const2),
            pl.BlockSpec((1, LANES), const2),
            pl.BlockSpec(cw.shape, const2),
            pl.BlockSpec((1, conv_dim), const2),
            pl.BlockSpec((1, conv_dim), const2),
        ],
        out_specs=(
            pl.BlockSpec((1, HEADS, tm, HEAD_PAD), lambda b, s: (b, 0, s, 0)),
            pl.BlockSpec((1, HEADS, tm, HEAD_PAD), lambda b, s: (b, 0, s, 0)),
            pl.BlockSpec((1, tm // TK, hv, TK), lambda b, s: (b, s, 0, 0)),
            pl.BlockSpec((1, tm, conv_dim), lambda b, s: (b, s, 0)),
        ),
        scratch_shapes=[pltpu.VMEM((tm + 2 * SUBLANES, conv_dim), F32)],
        compiler_params=pltpu.CompilerParams(
            dimension_semantics=("arbitrary", "arbitrary"), vmem_limit_bytes=VMEM_LIMIT),
        name="inproj",
    )(x, mod, pos, g1, win, gq, gkv, wq1, wq2, wk, wvt, invf, cw, cb, gc)


def _attn_kernel(q_ref, k_ref, vt_ref, o_ref, *, n_q):
    row = lax.broadcasted_iota(jnp.int32, (TK, TQ), 0)
    col = lax.broadcasted_iota(jnp.int32, (TK, TQ), 1)
    shift = CHUNK.bit_length() - 1
    diag_ok = (row >> shift) <= (col >> shift)

    def q_block(qi, carry):
        q0 = pl.multiple_of(qi * TQ, TQ)
        outs = []
        for hh in range(2):
            q = q_ref[0, hh, pl.ds(q0, TQ), :]

            def kv_step(j, state, masked):
                m, l, acc = state
                k0 = pl.multiple_of(j * TK, TK)
                k = k_ref[0, hh, pl.ds(k0, TK), :]
                s_t = _dot_nt(k, q)
                if masked:
                    s_t = jnp.where(diag_ok, s_t, NEG_INF)
                m_new = jnp.maximum(m, jnp.max(s_t, axis=0, keepdims=True))
                p = jnp.exp(s_t - m_new)
                alpha = jnp.exp(m - m_new)
                l_new = alpha * l + jnp.sum(p, axis=0, keepdims=True)
                v_t = vt_ref[0, j, hh * D_V:(hh + 1) * D_V, :]
                acc_new = alpha * acc + _dot(v_t, p.astype(BF16))
                return m_new, l_new, acc_new

            init = (jnp.full((1, TQ), NEG_INF, F32), jnp.zeros((1, TQ), F32),
                    jnp.zeros((D_V, TQ), F32))
            state = lax.fori_loop(0, qi, functools.partial(kv_step, masked=False), init)
            m, l, acc = kv_step(qi, state, True)
            outs.append(acc / l)
        o_t = jnp.concatenate(outs, axis=0)
        o_ref[0, pl.ds(q0, TQ), :] = o_t.T.astype(BF16)
        return carry

    lax.fori_loop(0, n_q, q_block, 0)


def _attention(q, k, vt):
    bsz, _, seq, _ = q.shape
    n_pair = HEADS // 2
    return pl.pallas_call(
        functools.partial(_attn_kernel, n_q=seq // TQ),
        out_shape=jax.ShapeDtypeStruct((bsz, seq, HEADS * D_V), BF16),
        grid=(bsz, n_pair),
        in_specs=[
            pl.BlockSpec((1, 2, seq, HEAD_PAD), lambda b, p: (b, p, 0, 0)),
            pl.BlockSpec((1, 2, seq, HEAD_PAD), lambda b, p: (b, p, 0, 0)),
            pl.BlockSpec((1, seq // TK, 2 * D_V, TK), lambda b, p: (b, 0, p, 0)),
        ],
        out_specs=pl.BlockSpec((1, seq, 2 * D_V), lambda b, p: (b, 0, p)),
        compiler_params=pltpu.CompilerParams(
            dimension_semantics=("arbitrary", "arbitrary"), vmem_limit_bytes=VMEM_LIMIT),
        name="attn",
    )(q, k, vt)


def _outproj_kernel(x_ref, attn_ref, conv_ref, mod_ref, ga_ref, wout_ref, g2_ref, wr_ref, br_ref,
                    tri_ref, x1_out, h2_out, route_out, cnt_out, base_ref):
    first = (pl.program_id(0) == 0) & (pl.program_id(1) == 0)

    @pl.when(first)
    def _():
        base_ref[...] = jnp.zeros(base_ref.shape, F32)

    mod = mod_ref[0]
    an = _rms(attn_ref[0].astype(F32), ga_ref[...]).astype(BF16)
    mixed = jnp.concatenate([an, conv_ref[0]], axis=-1)
    y = _dot(mixed, wout_ref[...])
    x1 = x_ref[0] + mod[2:3] * y
    x1_out[0] = x1
    h2 = _rms(x1, g2_ref[...]) * (1.0 + mod[4:5]) + mod[3:4]
    h2_out[0] = h2
    logits = _dot(h2.astype(BF16), wr_ref[...]) + br_ref[...]

    tm = logits.shape[0]
    lane = lax.broadcasted_iota(jnp.int32, (tm, LANES), 1).astype(F32)
    big = float(LANES)
    is_grp = (lane >= N_EXPERTS) & (lane < N_EXPERTS + N_GROUPS)
    gl = jnp.where(is_grp, logits, -jnp.inf)
    gmax = jnp.max(gl, axis=-1, keepdims=True)
    gidx = jnp.min(jnp.where(gl == gmax, lane, big), axis=-1, keepdims=True) - N_EXPERTS
    p_group = 1.0 / jnp.sum(jnp.where(is_grp, jnp.exp(logits - gmax), 0.0), axis=-1, keepdims=True)
    lo = gidx * EXPERTS_PER_GROUP
    in_grp = (lane >= lo) & (lane < lo + EXPERTS_PER_GROUP)
    el = jnp.where(in_grp, logits, -jnp.inf)
    v1 = jnp.max(el, axis=-1, keepdims=True)
    i1 = jnp.min(jnp.where(el == v1, lane, big), axis=-1, keepdims=True)
    el2 = jnp.where(lane == i1, -jnp.inf, el)
    v2 = jnp.max(el2, axis=-1, keepdims=True)
    i2 = jnp.min(jnp.where(el2 == v2, lane, big), axis=-1, keepdims=True)
    t = jnp.exp(v2 - v1)
    w1 = p_group / (1.0 + t)
    w2 = p_group * t / (1.0 + t)

    oh1 = lane == i1
    oh2 = lane == i2
    cnt = jnp.where(oh1 | oh2, 1.0, 0.0)
    prefix = _dot(tri_ref[...], cnt.astype(BF16)) + base_ref[0:1, :]
    r1 = jnp.sum(jnp.where(oh1, prefix, 0.0), axis=-1, keepdims=True)
    r2 = jnp.sum(jnp.where(oh2, prefix, 0.0), axis=-1, keepdims=True)
    new_base = base_ref[...] + jnp.sum(cnt, axis=0, keepdims=True)
    base_ref[...] = new_base
    cnt_out[...] = new_base

    route = jnp.where(lane == 0.0, i1, jnp.where(lane == 1.0, i2, jnp.where(
        lane == 2.0, r1, jnp.where(lane == 3.0, r2, jnp.where(
            lane == 4.0, w1, jnp.where(lane == 5.0, w2, 0.0))))))
    route_out[0] = route


def _outproj(x, attn, conv, mod, ga, wout, g2, wr, br):
    bsz, seq, d = x.shape
    tm = TM_OUT
    aw = attn.shape[-1]
    cwd = conv.shape[-1]
    tri = (lax.broadcasted_iota(jnp.int32, (tm, tm), 1)
           < lax.broadcasted_iota(jnp.int32, (tm, tm), 0)).astype(BF16)
    const2 = lambda b, s: (0, 0)
    tile3 = lambda b, s: (b, s, 0)
    return pl.pallas_call(
        _outproj_kernel,
        out_shape=(
            jax.ShapeDtypeStruct((bsz, seq, d), F32),
            jax.ShapeDtypeStruct((bsz, seq, d), F32),
            jax.ShapeDtypeStruct((bsz, seq, LANES), F32),
            jax.ShapeDtypeStruct((SUBLANES, LANES), F32),
        ),
        grid=(bsz, seq // tm),
        in_specs=[
            pl.BlockSpec((1, tm, d), tile3),
            pl.BlockSpec((1, tm, aw), tile3),
            pl.BlockSpec((1, tm, cwd), tile3),
            pl.BlockSpec((1, N_MOD, d), lambda b, s: (b, 0, 0)),
            pl.BlockSpec((1, aw), const2),
            pl.BlockSpec(wout.shape, const2),
            pl.BlockSpec((1, d), const2),
            pl.BlockSpec(wr.shape, const2),
            pl.BlockSpec((1, LANES), const2),
            pl.BlockSpec((tm, tm), const2),
        ],
        out_specs=(
            pl.BlockSpec((1, tm, d), tile3),
            pl.BlockSpec((1, tm, d), tile3),
            pl.BlockSpec((1, tm, LANES), tile3),
            pl.BlockSpec((SUBLANES, LANES), const2),
        ),
        scratch_shapes=[pltpu.VMEM((SUBLANES, LANES), F32)],
        compiler_params=pltpu.CompilerParams(
            dimension_semantics=("arbitrary", "arbitrary"), vmem_limit_bytes=VMEM_LIMIT),
        name="outproj",
    )(x, attn, conv, mod, ga, wout, g2, wr, br, tri)


def _dest_kernel(route_ref, cnt_ref, dest_out):
    cnt = cnt_ref[...]
    lane8 = lax.broadcasted_iota(jnp.int32, (SUBLANES, LANES), 1)
    inc = cnt
    shift = 1
    while shift < N_EXPERTS:
        inc = inc + jnp.where(lane8 >= shift, pltpu.roll(inc, shift, axis=1), 0.0)
        shift *= 2
    starts = (inc - cnt)[0:1, :]
    r = route_ref[...]
    tm = r.shape[0]
    lane = lax.broadcasted_iota(jnp.int32, (tm, LANES), 1).astype(F32)
    d1 = r[:, 2:3] + jnp.sum(jnp.where(lane == r[:, 0:1], starts, 0.0), axis=-1, keepdims=True)
    d2 = r[:, 3:4] + jnp.sum(jnp.where(lane == r[:, 1:2], starts, 0.0), axis=-1, keepdims=True)
    packed = jnp.where(lane == 0.0, d1, jnp.where(lane == 1.0, d2, 0.0))
    dest_out[...] = packed.T[0:SUBLANES, :].astype(jnp.int32)


def _dest(route, cnt):
    t = route.shape[0]
    tm = TM_DEST
    return pl.pallas_call(
        _dest_kernel,
        out_shape=jax.ShapeDtypeStruct((SUBLANES, t), jnp.int32),
        grid=(t // tm,),
        in_specs=[
            pl.BlockSpec((tm, LANES), lambda i: (i, 0)),
            pl.BlockSpec((SUBLANES, LANES), lambda i: (0, 0)),
        ],
        out_specs=pl.BlockSpec((SUBLANES, tm), lambda i: (0, i)),
        compiler_params=pltpu.CompilerParams(
            dimension_semantics=("arbitrary",), vmem_limit_bytes=VMEM_LIMIT),
        name="dest",
    )(route, cnt)


def _row_copy(src_hbm, src_row, dst_hbm, dst_row, sem):
    return pltpu.make_async_copy(src_hbm.at[pl.ds(src_row, 1)], dst_hbm.at[pl.ds(dst_row, 1)], sem)


def _dispatch_kernel(dest_ref, h_hbm, xs_hbm, sem, *, tm, n_steps):
    i = pl.program_id(0)

    def issue(t, carry):
        tok = i * tm + t
        for k in range(TOP_K):
            _row_copy(h_hbm, tok, xs_hbm, dest_ref[k, t], sem).start()
        return carry

    lax.fori_loop(0, tm, issue, 0, unroll=8)

    def drain(t, carry):
        for k in range(TOP_K):
            _row_copy(h_hbm, 0, xs_hbm, 0, sem).wait()
        return carry

    @pl.when(i > 0)
    def _():
        lax.fori_loop(0, tm, drain, 0, unroll=8)

    @pl.when(i == n_steps - 1)
    def _():
        lax.fori_loop(0, tm, drain, 0, unroll=8)


def _dispatch(dest, h):
    t, d = h.shape
    tm = TM_DISP
    n_steps = t // tm
    return pl.pallas_call(
        functools.partial(_dispatch_kernel, tm=tm, n_steps=n_steps),
        out_shape=jax.ShapeDtypeStruct((t * TOP_K, d), h.dtype),
        grid=(n_steps,),
        in_specs=[
            pl.BlockSpec((SUBLANES, tm), lambda i: (0, i), memory_space=pltpu.SMEM),
            pl.BlockSpec(memory_space=pl.ANY),
        ],
        out_specs=pl.BlockSpec(memory_space=pl.ANY),
        scratch_shapes=[pltpu.SemaphoreType.DMA],
        compiler_params=pltpu.CompilerParams(
            dimension_semantics=("arbitrary",)),
        name="dispatch",
    )(dest, h)


def _experts_kernel(blk_ref, exp_ref, lo_ref, hi_ref, x_ref, wg_ref, wu_ref, wd_ref, o_ref):
    i = pl.program_id(0)
    lo = lo_ref[i]
    hi = hi_ref[i]
    prev = jnp.maximum(i - 1, 0)
    first = (i == 0) | (blk_ref[i] != blk_ref[prev])

    @pl.when(first)
    def _():
        o_ref[...] = jnp.zeros(o_ref.shape, o_ref.dtype)

    @pl.when(hi > lo)
    def _():
        x = x_ref[...].astype(BF16)
        g = _dot(x, wg_ref[0])
        u = _dot(x, wu_ref[0])
        a = (g * jax.nn.sigmoid(g) * u).astype(BF16)
        y = _dot(a, wd_ref[0])
        row = lax.broadcasted_iota(jnp.int32, (x.shape[0], 1), 0)
        keep = (row >= lo) & (row < hi)
        o_ref[...] += jnp.where(keep, y, 0.0)


def _experts(item_blk, item_exp, item_lo, item_hi, xs, wg, wu, wd):
    a, d = xs.shape
    de = wg.shape[-1]
    n_items = item_blk.shape[0]
    grid_spec = pltpu.PrefetchScalarGridSpec(
        num_scalar_prefetch=4,
        grid=(n_items,),
        in_specs=[
            pl.BlockSpec((BLK, d), lambda i, blk, ex, lo, hi: (blk[i], 0)),
            pl.BlockSpec((1, d, de), lambda i, blk, ex, lo, hi: (ex[i], 0, 0)),
            pl.BlockSpec((1, d, de), lambda i, blk, ex, lo, hi: (ex[i], 0, 0)),
            pl.BlockSpec((1, de, d), lambda i, blk, ex, lo, hi: (ex[i], 0, 0)),
        ],
        out_specs=pl.BlockSpec((BLK, d), lambda i, blk, ex, lo, hi: (blk[i], 0)),
    )
    return pl.pallas_call(
        _experts_kernel,
        out_shape=jax.ShapeDtypeStruct((a, d), F32),
        grid_spec=grid_spec,
        compiler_params=pltpu.CompilerParams(
            dimension_semantics=("arbitrary",), vmem_limit_bytes=VMEM_LIMIT),
        name="experts",
    )(item_blk, item_exp, item_lo, item_hi, xs, wg, wu, wd)


def _work_items(counts, n_rows):
    n_blocks = n_rows // BLK
    n_items = n_blocks + N_EXPERTS - 1
    counts = counts.astype(jnp.int32)
    ends = jnp.cumsum(counts)
    starts = ends - counts
    first_blk = starts // BLK
    last_blk = jnp.maximum(ends - 1, 0) // BLK
    n_e = jnp.where(counts > 0, last_blk - first_blk + 1, 0)
    item_end = jnp.cumsum(n_e)
    item_start = item_end - n_e
    idx = jnp.arange(n_items, dtype=jnp.int32)
    total = item_end[-1]
    ex = jnp.minimum(jnp.searchsorted(item_end, idx, side="right"), N_EXPERTS - 1).astype(jnp.int32)
    blk = first_blk[ex] + (idx - item_start[ex])
    lo = jnp.maximum(starts[ex], blk * BLK) - blk * BLK
    hi = jnp.minimum(ends[ex], (blk + 1) * BLK) - blk * BLK
    valid = idx < total
    last_ex = ex[jnp.maximum(total - 1, 0)]
    blk = jnp.where(valid, blk, n_blocks - 1).astype(jnp.int32)
    ex = jnp.where(valid, ex, last_ex).astype(jnp.int32)
    lo = jnp.where(valid, lo, 0).astype(jnp.int32)
    hi = jnp.where(valid, hi, 0).astype(jnp.int32)
    return blk, ex, lo, hi


def _combine_kernel(dest_ref, dest_next_ref, ys_hbm, x1_ref, route_ref, mod_ref, gf_ref, o_ref,
                    ybuf, sems, *, tm, n_steps, final_norm):
    i = pl.program_id(0)
    slot = i % 2

    def gather(d_ref, slot_idx):
        def body(t, carry):
            for k in range(TOP_K):
                pltpu.make_async_copy(ys_hbm.at[pl.ds(d_ref[k, t], 1)],
                                      ybuf.at[slot_idx, k, pl.ds(t, 1)], sems.at[slot_idx]).start()
            return carry
        lax.fori_loop(0, tm, body, 0, unroll=8)

    @pl.when(i == 0)
    def _():
        gather(dest_ref, 0)

    @pl.when(i + 1 < n_steps)
    def _():
        gather(dest_next_ref, 1 - slot)

    def drain(t, carry):
        for k in range(TOP_K):
            pltpu.make_async_copy(ys_hbm.at[pl.ds(0, 1)], ybuf.at[slot, k, pl.ds(0, 1)],
                                  sems.at[slot]).wait()
        return carry

    lax.fori_loop(0, tm, drain, 0, unroll=8)

    r = route_ref[...]
    y = r[:, 4:5] * ybuf[slot, 0] + r[:, 5:6] * ybuf[slot, 1]
    mod = mod_ref[0]
    x2 = x1_ref[...] + mod[5:6] * y
    o_ref[...] = _rms(x2, gf_ref[...]) if final_norm else x2


def _combine(dest, ys, x1, route, mod, gf, seq, final_norm):
    t, d = x1.shape
    tm = TM_COMB
    n_steps = t // tm
    per_b = seq // tm
    return pl.pallas_call(
        functools.partial(_combine_kernel, tm=tm, n_steps=n_steps, final_norm=final_norm),
        out_shape=jax.ShapeDtypeStruct((t, d), F32),
        grid=(n_steps,),
        in_specs=[
            pl.BlockSpec((SUBLANES, tm), lambda i: (0, i), memory_space=pltpu.SMEM),
            pl.BlockSpec((SUBLANES, tm), lambda i: (0, jnp.minimum(i + 1, n_steps - 1)),
                         memory_space=pltpu.SMEM),
            pl.BlockSpec(memory_space=pl.ANY),
            pl.BlockSpec((tm, d), lambda i: (i, 0)),
            pl.BlockSpec((tm, LANES), lambda i: (i, 0)),
            pl.BlockSpec((1, N_MOD, d), lambda i: (i // per_b, 0, 0)),
            pl.BlockSpec((1, d), lambda i: (0, 0)),
        ],
        out_specs=pl.BlockSpec((tm, d), lambda i: (i, 0)),
        scratch_shapes=[pltpu.VMEM((2, TOP_K, tm, d), F32), pltpu.SemaphoreType.DMA((2,))],
        compiler_params=pltpu.CompilerParams(
            dimension_semantics=("arbitrary",), vmem_limit_bytes=VMEM_LIMIT),
        name="combine",
    )(dest, dest, ys, x1, route, mod, gf)


def _prep_in_weights(w_in, q_rank, kv_rank, conv_dim):
    o1 = q_rank
    o2 = o1 + kv_rank
    o3 = o2 + D_ROPE
    d = w_in.shape[0]
    w_kr = w_in[:, o2:o3]
    half = D_ROPE // 2
    w_kr_rot = jnp.concatenate([-w_kr[:, half:], w_kr[:, :half]], axis=1)
    zl = jnp.zeros((d, D_NOPE), w_in.dtype)
    zr = jnp.zeros((d, LANES - D_NOPE - D_ROPE), w_in.dtype)
    return jnp.concatenate(
        [w_in[:, :o2], zl, w_kr, zr, zl, w_kr_rot, zr, w_in[:, o3:]], axis=1).astype(BF16)


def _prep_q_weights(w_uq):
    r = w_uq.shape[0]
    w = w_uq.reshape(r, HEADS, D_NOPE + D_ROPE)
    nope, rope = w[..., :D_NOPE], w[..., D_NOPE:]
    half = D_ROPE // 2
    rot = jnp.concatenate([-rope[..., half:], rope[..., :half]], axis=-1)
    zr = jnp.zeros((r, HEADS, HEAD_PAD - D_NOPE - D_ROPE), w_uq.dtype)
    w1 = jnp.concatenate([nope, rope, zr], axis=-1).reshape(r, HEADS * HEAD_PAD)
    w2 = jnp.concatenate([jnp.zeros_like(nope), rot, zr], axis=-1).reshape(r, HEADS * HEAD_PAD)
    return w1.astype(BF16), w2.astype(BF16)


def _prep_k_weights(w_uk):
    r = w_uk.shape[0]
    w = w_uk.reshape(r, HEADS, D_NOPE)
    zr = jnp.zeros((r, HEADS, HEAD_PAD - D_NOPE), w_uk.dtype)
    return jnp.concatenate([w, zr], axis=-1).reshape(r, HEADS * HEAD_PAD).astype(BF16)


def _rope_lane_freqs():
    inv_freq = 1.0 / (ROPE_THETA ** (jnp.arange(0, D_ROPE, 2, dtype=F32) / D_ROPE))
    z = jnp.zeros((D_NOPE,), F32)
    zr = jnp.zeros((LANES - D_NOPE - D_ROPE,), F32)
    return jnp.concatenate([z, inv_freq, inv_freq, zr]).reshape(1, LANES)


def kernel(x, c, positions, w_ada, b_ada, norm1_g, w_in, q_norm_g, kv_norm_g, w_uq, w_uk, w_uv,
           conv_w, conv_b, attn_out_g, conv_out_g, w_out, norm2_g, w_router_group, b_router_group,
           w_router_expert, b_router_expert, w_exp_gate, w_exp_up, w_exp_down, final_g):
    bsz, seq, d = x.shape
    depth = w_ada.shape[0]
    q_rank = q_norm_g.shape[-1]
    kv_rank = kv_norm_g.shape[-1]
    conv_dim = conv_w.shape[-1]
    t = bsz * seq
    assert seq % max(TM_PROJ, TQ, TM_OUT, TM_COMB) == 0 and TQ == TK and TM_PROJ % TK == 0
    assert CHUNK & (CHUNK - 1) == 0 and TQ % CHUNK == 0
    assert t % max(TM_DEST, TM_DISP, TM_COMB) == 0 and (t * TOP_K) % BLK == 0
    assert w_uq.shape[-1] == HEADS * (D_NOPE + D_ROPE) and w_uv.shape[-1] == HEADS * D_V

    pos = positions.reshape(bsz, seq, 1)
    invf = _rope_lane_freqs()
    row = lambda v: v.reshape(1, -1)
    for l in range(depth):
        mod = _adaln(c, w_ada[l], b_ada[l]).reshape(bsz, N_MOD, d)
        win = _prep_in_weights(w_in[l], q_rank, kv_rank, conv_dim)
        wq1, wq2 = _prep_q_weights(w_uq[l])
        wk = _prep_k_weights(w_uk[l])
        wvt = w_uv[l].T.astype(BF16)
        q, k, vt, conv = _inproj(x, mod, pos, row(norm1_g[l]), win, row(q_norm_g[l]),
                                 row(kv_norm_g[l]), wq1, wq2, wk, wvt, invf, conv_w[l],
                                 row(conv_b[l]), row(conv_out_g[l]))
        attn = _attention(q, k, vt)

        wr = jnp.concatenate(
            [w_router_expert[l], w_router_group[l],
             jnp.zeros((d, LANES - N_EXPERTS - N_GROUPS), F32)], axis=1).astype(BF16)
        br = jnp.concatenate(
            [b_router_expert[l], b_router_group[l],
             jnp.zeros((LANES - N_EXPERTS - N_GROUPS,), F32)]).reshape(1, LANES)
        x1, h2, route, cnt = _outproj(x, attn, conv, mod, row(attn_out_g[l]),
                                      w_out[l].astype(BF16), row(norm2_g[l]), wr, br)
        route = route.reshape(t, LANES)
        dest = _dest(route, cnt)
        xs = _dispatch(dest, h2.reshape(t, d))
        items = _work_items(cnt[0, :N_EXPERTS], t * TOP_K)
        ys = _experts(*items, xs, w_exp_gate[l].astype(BF16), w_exp_up[l].astype(BF16),
                      w_exp_down[l].astype(BF16))
        out = _combine(dest, ys, x1.reshape(t, d), route, mod, row(final_g), seq, l == depth - 1)
        x = out.reshape(bsz, seq, d)
    return x
```

```python
import functools
import math

import jax
import jax.numpy as jnp
from jax import lax
from jax.experimental import pallas as pl
from jax.experimental.pallas import tpu as pltpu

F32 = jnp.float32
BF16 = jnp.bfloat16

CHUNK = 64
HEADS = 8
D_NOPE = 64
D_ROPE = 32
D_V = 64
ROPE_THETA = 10000.0
CONV_WIDTH = 3
N_GROUPS = 4
EXPERTS_PER_GROUP = 8
N_EXPERTS = N_GROUPS * EXPERTS_PER_GROUP
TOP_K = 2
N_MOD = 6
RMS_EPS = 1e-6
NEG_INF = -1e30

LANES = 128
SUBLANES = 8
HEAD_PAD = LANES
VMEM_LIMIT = 56 * 1024 * 1024

TM_PROJ = 256
TQ = 512
TK = 256
TM_OUT = 256
TM_DEST = 512
TM_DISP = 512
BLK = 256
TM_COMB = 256
TN_ADA = 512


def _rms(x, g):
    return x * lax.rsqrt(jnp.mean(x * x, axis=-1, keepdims=True) + RMS_EPS) * g


def _dot(a, b):
    return jnp.dot(a, b, preferred_element_type=F32)


def _dot_nt(a, b):
    return lax.dot_general(a, b, (((1,), (1,)), ((), ())), preferred_element_type=F32)


def _split_bf16(x):
    hi = x.astype(BF16)
    lo = (x - hi.astype(F32)).astype(BF16)
    return hi, lo


def _adaln_kernel(c_ref, w_ref, b_ref, o_ref):
    c = c_ref[...]
    ca = c * jax.nn.sigmoid(c)
    c_hi, c_lo = _split_bf16(ca)
    w_hi, w_lo = _split_bf16(w_ref[...])
    acc = _dot(c_hi, w_hi) + _dot(c_hi, w_lo) + _dot(c_lo, w_hi)
    o_ref[...] = acc + b_ref[...]


def _adaln(c, w, b):
    bsz, d = c.shape
    n = w.shape[1]
    return pl.pallas_call(
        _adaln_kernel,
        out_shape=jax.ShapeDtypeStruct((bsz, n), F32),
        grid=(n // TN_ADA,),
        in_specs=[
            pl.BlockSpec((bsz, d), lambda j: (0, 0)),
            pl.BlockSpec((d, TN_ADA), lambda j: (0, j)),
            pl.BlockSpec((1, TN_ADA), lambda j: (0, j)),
        ],
        out_specs=pl.BlockSpec((bsz, TN_ADA), lambda j: (0, j)),
        compiler_params=pltpu.CompilerParams(
            dimension_semantics=("arbitrary",), vmem_limit_bytes=VMEM_LIMIT),
        name="adaln",
    )(c, w, b.reshape(1, n))


def _inproj_kernel(x_ref, mod_ref, pos_ref, g1_ref, win_ref, gq_ref, gkv_ref, wq1_ref, wq2_ref,
                   wk_ref, wvt_ref, invf_ref, cw_ref, cb_ref, gc_ref,
                   q_out, k_out, vt_out, conv_out, zbuf, *, tm, q_rank, kv_rank, conv_dim):
    si = pl.program_id(1)
    x = x_ref[0]
    mod = mod_ref[0]
    h = (_rms(x, g1_ref[...]) * (1.0 + mod[1:2]) + mod[0:1]).astype(BF16)
    proj = _dot(h, win_ref[...])

    o_kv = q_rank
    o_kr = o_kv + kv_rank
    o_u = o_kr + 2 * LANES
    q_c = proj[:, 0:q_rank]
    kv_c = proj[:, o_kv:o_kv + kv_rank]
    kr = proj[:, o_kr:o_kr + LANES]
    kr_rot = proj[:, o_kr + LANES:o_kr + 2 * LANES]
    u = proj[:, o_u:o_u + conv_dim]
    b_gate = proj[:, o_u + conv_dim:o_u + 2 * conv_dim]
    c_gate = proj[:, o_u + 2 * conv_dim:o_u + 3 * conv_dim]

    ang = pos_ref[0].astype(F32) * invf_ref[...]
    cos_t = jnp.cos(ang)
    sin_t = jnp.sin(ang)
    lane = lax.broadcasted_iota(jnp.int32, (1, LANES), 1)
    is_rope = (lane >= D_NOPE) & (lane < D_NOPE + D_ROPE)
    sm_scale = 1.0 / math.sqrt(D_NOPE + D_ROPE)
    q_cos = jnp.where(lane < D_NOPE, sm_scale, jnp.where(is_rope, cos_t * sm_scale, 0.0))
    q_sin = jnp.where(is_rope, sin_t * sm_scale, 0.0)

    qn = _rms(q_c, gq_ref[...]).astype(BF16)
    qa = _dot(qn, wq1_ref[...])
    qb = _dot(qn, wq2_ref[...])
    kvn = _rms(kv_c, gkv_ref[...]).astype(BF16)
    kn = _dot(kvn, wk_ref[...])
    k_rope = kr * cos_t + kr_rot * sin_t
    for hd in range(HEADS):
        sl = slice(hd * HEAD_PAD, (hd + 1) * HEAD_PAD)
        q_out[0, hd] = (qa[:, sl] * q_cos + qb[:, sl] * q_sin).astype(BF16)
        k_out[0, hd] = (kn[:, sl] + k_rope).astype(BF16)

    vt = _dot_nt(wvt_ref[...], kvn)
    for j in range(tm // TK):
        vt_out[0, j] = vt[:, j * TK:(j + 1) * TK].astype(BF16)

    z = c_gate * u

    @pl.when(si == 0)
    def _():
        zbuf[0:SUBLANES, :] = jnp.zeros((SUBLANES, conv_dim), F32)

    zbuf[SUBLANES:SUBLANES + tm, :] = z
    z1 = zbuf[SUBLANES - 1:SUBLANES - 1 + tm, :]
    z2 = zbuf[SUBLANES - 2:SUBLANES - 2 + tm, :]
    cw = cw_ref[...]
    zc = cw[0:1] * z2 + cw[1:2] * z1 + cw[2:3] * z + cb_ref[...]
    conv = b_gate * zc
    conv_out[0] = _rms(conv, gc_ref[...]).astype(BF16)
    zbuf[0:SUBLANES, :] = zbuf[tm:tm + SUBLANES, :]


def _inproj(x, mod, pos, g1, win, gq, gkv, wq1, wq2, wk, wvt, invf, cw, cb, gc):
    bsz, seq, d = x.shape
    tm = TM_PROJ
    q_rank, kv_rank = wq1.shape[0], wk.shape[0]
    conv_dim = cw.shape[1]
    hv = wvt.shape[0]
    n_s = seq // tm
    const2 = lambda b, s: (0, 0)
    kern = functools.partial(_inproj_kernel, tm=tm, q_rank=q_rank, kv_rank=kv_rank,
                             conv_dim=conv_dim)
    return pl.pallas_call(
        kern,
        out_shape=(
            jax.ShapeDtypeStruct((bsz, HEADS, seq, HEAD_PAD), BF16),
            jax.ShapeDtypeStruct((bsz, HEADS, seq, HEAD_PAD), BF16),
            jax.ShapeDtypeStruct((bsz, seq // TK, hv, TK), BF16),
            jax.ShapeDtypeStruct((bsz, seq, conv_dim), BF16),
        ),
        grid=(bsz, n_s),
        in_specs=[
            pl.BlockSpec((1, tm, d), lambda b, s: (b, s, 0)),
            pl.BlockSpec((1, N_MOD, d), lambda b, s: (b, 0, 0)),
            pl.BlockSpec((1, tm, 1), lambda b, s: (b, s, 0)),
            pl.BlockSpec((1, d), const2),
            pl.BlockSpec(win.shape, const2),
            pl.BlockSpec((1, q_rank), const2),
            pl.BlockSpec((1, kv_rank), const2),
            pl.BlockSpec(wq1.shape, const2),
            pl.BlockSpec(wq2.shape, const2),
            pl.BlockSpec(wk.shape, const2),
            pl.BlockSpec(wvt.shape, const2),
            pl.BlockSpec((1, LANES), const2),
            pl.BlockSpec(cw.shape, const2),
            pl.BlockSpec((1, conv_dim), const2),
            pl.BlockSpec((1, conv_dim), const2),
        ],
        out_specs=(
            pl.BlockSpec((1, HEADS, tm, HEAD_PAD), lambda b, s: (b, 0, s, 0)),
            pl.BlockSpec((1, HEADS, tm, HEAD_PAD), lambda b, s: (b, 0, s, 0)),
            pl.BlockSpec((1, tm // TK, hv, TK), lambda b, s: (b, s, 0, 0)),
            pl.BlockSpec((1, tm, conv_dim), lambda b, s: (b, s, 0)),
        ),
        scratch_shapes=[pltpu.VMEM((tm + 2 * SUBLANES, conv_dim), F32)],
        compiler_params=pltpu.CompilerParams(
            dimension_semantics=("arbitrary", "arbitrary"), vmem_limit_bytes=VMEM_LIMIT),
        name="inproj",
    )(x, mod, pos, g1, win, gq, gkv, wq1, wq2, wk, wvt, invf, cw, cb, gc)


def _attn_kernel(q_ref, k_ref, vt_ref, o_ref, *, n_q):
    row = lax.broadcasted_iota(jnp.int32, (TK, TQ), 0)
    col = lax.broadcasted_iota(jnp.int32, (TK, TQ), 1)
    shift = CHUNK.bit_length() - 1
    n_diag = TQ // TK

    def q_block(qi, carry):
        q0 = pl.multiple_of(qi * TQ, TQ)

        def head_step(hh, j, state, diag):
            m, l, acc = state
            k0 = pl.multiple_of(j * TK, TK)
            q = q_ref[0, hh, pl.ds(q0, TQ), :]
            k = k_ref[0, hh, pl.ds(k0, TK), :]
            s_t = _dot_nt(k, q)
            if diag is not None:
                ok = ((row + diag * TK) >> shift) <= (col >> shift)
                s_t = jnp.where(ok, s_t, NEG_INF)
            m_new = jnp.maximum(m, jnp.max(s_t, axis=0, keepdims=True))
            p = jnp.exp(s_t - m_new)
            alpha = jnp.exp(m - m_new)
            l_new = alpha * l + jnp.sum(p, axis=0, keepdims=True)
            v_t = vt_ref[0, j, hh * D_V:(hh + 1) * D_V, :]
            acc_new = alpha * acc + _dot(v_t, p.astype(BF16))
            return m_new, l_new, acc_new

        def both_heads(j, states, diag=None):
            return tuple(head_step(hh, j, states[hh], diag) for hh in range(2))

        init = (jnp.full((1, TQ), NEG_INF, F32), jnp.zeros((1, TQ), F32),
                jnp.zeros((D_V, TQ), F32))
        j_diag = qi * n_diag
        states = lax.fori_loop(0, j_diag, both_heads, (init, init))
        for dg in range(n_diag):
            states = both_heads(j_diag + dg, states, diag=dg)
        o_t = jnp.concatenate([acc / l for (_, l, acc) in states], axis=0)
        o_ref[0, pl.ds(q0, TQ), :] = o_t.T.astype(BF16)
        return carry

    lax.fori_loop(0, n_q, q_block, 0)


def _attention(q, k, vt):
    bsz, _, seq, _ = q.shape
    n_pair = HEADS // 2
    return pl.pallas_call(
        functools.partial(_attn_kernel, n_q=seq // TQ),
        out_shape=jax.ShapeDtypeStruct((bsz, seq, HEADS * D_V), BF16),
        grid=(bsz, n_pair),
        in_specs=[
            pl.BlockSpec((1, 2, seq, HEAD_PAD), lambda b, p: (b, p, 0, 0)),
            pl.BlockSpec((1, 2, seq, HEAD_PAD), lambda b, p: (b, p, 0, 0)),
            pl.BlockSpec((1, seq // TK, 2 * D_V, TK), lambda b, p: (b, 0, p, 0)),
        ],
        out_specs=pl.BlockSpec((1, seq, 2 * D_V), lambda b, p: (b, 0, p)),
        compiler_params=pltpu.CompilerParams(
            dimension_semantics=("arbitrary", "arbitrary"), vmem_limit_bytes=VMEM_LIMIT),
        name="attn",
    )(q, k, vt)


def _outproj_kernel(x_ref, attn_ref, conv_ref, mod_ref, ga_ref, wout_ref, g2_ref, wr_ref, br_ref,
                    tri_ref, x1_out, h2_out, route_out, cnt_out, base_ref):
    first = (pl.program_id(0) == 0) & (pl.program_id(1) == 0)

    @pl.when(first)
    def _():
        base_ref[...] = jnp.zeros(base_ref.shape, F32)

    mod = mod_ref[0]
    an = _rms(attn_ref[0].astype(F32), ga_ref[...]).astype(BF16)
    mixed = jnp.concatenate([an, conv_ref[0]], axis=-1)
    y = _dot(mixed, wout_ref[...])
    x1 = x_ref[0] + mod[2:3] * y
    x1_out[0] = x1
    h2 = _rms(x1, g2_ref[...]) * (1.0 + mod[4:5]) + mod[3:4]
    h2_out[0] = h2
    logits = _dot(h2.astype(BF16), wr_ref[...]) + br_ref[...]

    tm = logits.shape[0]
    lane = lax.broadcasted_iota(jnp.int32, (tm, LANES), 1).astype(F32)
    big = float(LANES)
    is_grp = (lane >= N_EXPERTS) & (lane < N_EXPERTS + N_GROUPS)
    gl = jnp.where(is_grp, logits, -jnp.inf)
    gmax = jnp.max(gl, axis=-1, keepdims=True)
    gidx = jnp.min(jnp.where(gl == gmax, lane, big), axis=-1, keepdims=True) - N_EXPERTS
    p_group = 1.0 / jnp.sum(jnp.where(is_grp, jnp.exp(logits - gmax), 0.0), axis=-1, keepdims=True)
    lo = gidx * EXPERTS_PER_GROUP
    in_grp = (lane >= lo) & (lane < lo + EXPERTS_PER_GROUP)
    el = jnp.where(in_grp, logits, -jnp.inf)
    v1 = jnp.max(el, axis=-1, keepdims=True)
    i1 = jnp.min(jnp.where(el == v1, lane, big), axis=-1, keepdims=True)
    el2 = jnp.where(lane == i1, -jnp.inf, el)
    v2 = jnp.max(el2, axis=-1, keepdims=True)
    i2 = jnp.min(jnp.where(el2 == v2, lane, big), axis=-1, keepdims=True)
    t = jnp.exp(v2 - v1)
    w1 = p_group / (1.0 + t)
    w2 = p_group * t / (1.0 + t)

    oh1 = lane == i1
    oh2 = lane == i2
    cnt = jnp.where(oh1 | oh2, 1.0, 0.0)
    prefix = _dot(tri_ref[...], cnt.astype(BF16)) + base_ref[0:1, :]
    r1 = jnp.sum(jnp.where(oh1, prefix, 0.0), axis=-1, keepdims=True)
    r2 = jnp.sum(jnp.where(oh2, prefix, 0.0), axis=-1, keepdims=True)
    new_base = base_ref[...] + jnp.sum(cnt, axis=0, keepdims=True)
    base_ref[...] = new_base
    cnt_out[...] = new_base

    route = jnp.where(lane == 0.0, i1, jnp.where(lane == 1.0, i2, jnp.where(
        lane == 2.0, r1, jnp.where(lane == 3.0, r2, jnp.where(
            lane == 4.0, w1, jnp.where(lane == 5.0, w2, 0.0))))))
    route_out[0] = route


def _outproj(x, attn, conv, mod, ga, wout, g2, wr, br):
    bsz, seq, d = x.shape
    tm = TM_OUT
    aw = attn.shape[-1]
    cwd = conv.shape[-1]
    tri = (lax.broadcasted_iota(jnp.int32, (tm, tm), 1)
           < lax.broadcasted_iota(jnp.int32, (tm, tm), 0)).astype(BF16)
    const2 = lambda b, s: (0, 0)
    tile3 = lambda b, s: (b, s, 0)
    return pl.pallas_call(
        _outproj_kernel,
        out_shape=(
            jax.ShapeDtypeStruct((bsz, seq, d), F32),
            jax.ShapeDtypeStruct((bsz, seq, d), F32),
            jax.ShapeDtypeStruct((bsz, seq, LANES), F32),
            jax.ShapeDtypeStruct((SUBLANES, LANES), F32),
        ),
        grid=(bsz, seq // tm),
        in_specs=[
            pl.BlockSpec((1, tm, d), tile3),
            pl.BlockSpec((1, tm, aw), tile3),
            pl.BlockSpec((1, tm, cwd), tile3),
            pl.BlockSpec((1, N_MOD, d), lambda b, s: (b, 0, 0)),
            pl.BlockSpec((1, aw), const2),
            pl.BlockSpec(wout.shape, const2),
            pl.BlockSpec((1, d), const2),
            pl.BlockSpec(wr.shape, const2),
            pl.BlockSpec((1, LANES), const2),
            pl.BlockSpec((tm, tm), const2),
        ],
        out_specs=(
            pl.BlockSpec((1, tm, d), tile3),
            pl.BlockSpec((1, tm, d), tile3),
            pl.BlockSpec((1, tm, LANES), tile3),
            pl.BlockSpec((SUBLANES, LANES), const2),
        ),
        scratch_shapes=[pltpu.VMEM((SUBLANES, LANES), F32)],
        compiler_params=pltpu.CompilerParams(
            dimension_semantics=("arbitrary", "arbitrary"), vmem_limit_bytes=VMEM_LIMIT),
        name="outproj",
    )(x, attn, conv, mod, ga, wout, g2, wr, br, tri)


def _dest_kernel(route_ref, cnt_ref, dest_out):
    cnt = cnt_ref[...]
    lane8 = lax.broadcasted_iota(jnp.int32, (SUBLANES, LANES), 1)
    inc = cnt
    shift = 1
    while shift < N_EXPERTS:
        inc = inc + jnp.where(lane8 >= shift, pltpu.roll(inc, shift, axis=1), 0.0)
        shift *= 2
    starts = (inc - cnt)[0:1, :]
    r = route_ref[...]
    tm = r.shape[0]
    lane = lax.broadcasted_iota(jnp.int32, (tm, LANES), 1).astype(F32)
    d1 = r[:, 2:3] + jnp.sum(jnp.where(lane == r[:, 0:1], starts, 0.0), axis=-1, keepdims=True)
    d2 = r[:, 3:4] + jnp.sum(jnp.where(lane == r[:, 1:2], starts, 0.0), axis=-1, keepdims=True)
    packed = jnp.where(lane == 0.0, d1, jnp.where(lane == 1.0, d2, 0.0))
    dest_out[...] = packed.T[0:SUBLANES, :].astype(jnp.int32)


def _dest(route, cnt):
    t = route.shape[0]
    tm = TM_DEST
    return pl.pallas_call(
        _dest_kernel,
        out_shape=jax.ShapeDtypeStruct((SUBLANES, t), jnp.int32),
        grid=(t // tm,),
        in_specs=[
            pl.BlockSpec((tm, LANES), lambda i: (i, 0)),
            pl.BlockSpec((SUBLANES, LANES), lambda i: (0, 0)),
        ],
        out_specs=pl.BlockSpec((SUBLANES, tm), lambda i: (0, i)),
        compiler_params=pltpu.CompilerParams(
            dimension_semantics=("arbitrary",), vmem_limit_bytes=VMEM_LIMIT),
        name="dest",
    )(route, cnt)


def _row_copy(src, src_row, dst, dst_row, sem):
    return pltpu.make_async_copy(src.at[pl.ds(src_row, 1)], dst.at[pl.ds(dst_row, 1)], sem)


def _dispatch_kernel(dest_ref, h_ref, xs_hbm, sem, *, tm):
    def issue(t, carry):
        for k in range(TOP_K):
            _row_copy(h_ref, t, xs_hbm, dest_ref[k, t], sem).start()
        return carry

    lax.fori_loop(0, tm, issue, 0, unroll=8)

    def drain(t, carry):
        for k in range(TOP_K):
            _row_copy(h_ref, 0, xs_hbm, 0, sem).wait()
        return carry

    lax.fori_loop(0, tm, drain, 0, unroll=8)


def _dispatch(dest, h):
    t, d = h.shape
    tm = TM_DISP
    return pl.pallas_call(
        functools.partial(_dispatch_kernel, tm=tm),
        out_shape=jax.ShapeDtypeStruct((t * TOP_K, d), h.dtype),
        grid=(t // tm,),
        in_specs=[
            pl.BlockSpec((SUBLANES, tm), lambda i: (0, i), memory_space=pltpu.SMEM),
            pl.BlockSpec((tm, d), lambda i: (i, 0)),
        ],
        out_specs=pl.BlockSpec(memory_space=pl.ANY),
        scratch_shapes=[pltpu.SemaphoreType.DMA],
        compiler_params=pltpu.CompilerParams(
            dimension_semantics=("arbitrary",), vmem_limit_bytes=VMEM_LIMIT),
        name="dispatch",
    )(dest, h)


def _experts_kernel(blk_ref, exp_ref, lo_ref, hi_ref, x_ref, wg_ref, wu_ref, wd_ref, o_ref):
    i = pl.program_id(0)
    lo = lo_ref[i]
    hi = hi_ref[i]
    prev = jnp.maximum(i - 1, 0)
    first = (i == 0) | (blk_ref[i] != blk_ref[prev])

    @pl.when(first)
    def _():
        o_ref[...] = jnp.zeros(o_ref.shape, o_ref.dtype)

    @pl.when(hi > lo)
    def _():
        x = x_ref[...].astype(BF16)
        g = _dot(x, wg_ref[0])
        u = _dot(x, wu_ref[0])
        a = (g * jax.nn.sigmoid(g) * u).astype(BF16)
        y = _dot(a, wd_ref[0])
        row = lax.broadcasted_iota(jnp.int32, (x.shape[0], 1), 0)
        keep = (row >= lo) & (row < hi)
        o_ref[...] += jnp.where(keep, y, 0.0)


def _experts(item_blk, item_exp, item_lo, item_hi, xs, wg, wu, wd):
    a, d = xs.shape
    de = wg.shape[-1]
    n_items = item_blk.shape[0]
    grid_spec = pltpu.PrefetchScalarGridSpec(
        num_scalar_prefetch=4,
        grid=(n_items,),
        in_specs=[
            pl.BlockSpec((BLK, d), lambda i, blk, ex, lo, hi: (blk[i], 0)),
            pl.BlockSpec((1, d, de), lambda i, blk, ex, lo, hi: (ex[i], 0, 0)),
            pl.BlockSpec((1, d, de), lambda i, blk, ex, lo, hi: (ex[i], 0, 0)),
            pl.BlockSpec((1, de, d), lambda i, blk, ex, lo, hi: (ex[i], 0, 0)),
        ],
        out_specs=pl.BlockSpec((BLK, d), lambda i, blk, ex, lo, hi: (blk[i], 0)),
    )
    return pl.pallas_call(
        _experts_kernel,
        out_shape=jax.ShapeDtypeStruct((a, d), F32),
        grid_spec=grid_spec,
        compiler_params=pltpu.CompilerParams(
            dimension_semantics=("arbitrary",), vmem_limit_bytes=VMEM_LIMIT),
        name="experts",
    )(item_blk, item_exp, item_lo, item_hi, xs, wg, wu, wd)


def _work_items(counts, n_rows):
    n_blocks = n_rows // BLK
    n_items = n_blocks + N_EXPERTS - 1
    counts = counts.astype(jnp.int32)
    ends = jnp.cumsum(counts)
    starts = ends - counts
    first_blk = starts // BLK
    last_blk = jnp.maximum(ends - 1, 0) // BLK
    n_e = jnp.where(counts > 0, last_blk - first_blk + 1, 0)
    item_end = jnp.cumsum(n_e)
    item_start = item_end - n_e
    idx = jnp.arange(n_items, dtype=jnp.int32)
    total = item_end[-1]
    ex = jnp.minimum(jnp.sum((idx[:, None] >= item_end[None, :]).astype(jnp.int32), axis=1),
                     N_EXPERTS - 1)
    onehot = (ex[:, None] == jnp.arange(N_EXPERTS, dtype=jnp.int32)[None, :]).astype(jnp.int32)
    pick = lambda table: jnp.sum(onehot * table[None, :], axis=1)
    blk = pick(first_blk) + (idx - pick(item_start))
    lo = jnp.maximum(pick(starts), blk * BLK) - blk * BLK
    hi = jnp.minimum(pick(ends), (blk + 1) * BLK) - blk * BLK
    valid = idx < total
    last_ex = jnp.max(jnp.where(valid, ex, 0))
    blk = jnp.where(valid, blk, n_blocks - 1).astype(jnp.int32)
    ex = jnp.where(valid, ex, last_ex).astype(jnp.int32)
    lo = jnp.where(valid, lo, 0).astype(jnp.int32)
    hi = jnp.where(valid, hi, 0).astype(jnp.int32)
    return blk, ex, lo, hi


def _combine_kernel(dest_ref, dest_next_ref, ys_hbm, x1_ref, route_ref, mod_ref, gf_ref, o_ref,
                    ybuf, sems, *, tm, n_steps, final_norm):
    i = pl.program_id(0)
    slot = i % 2

    def gather(d_ref, slot_idx):
        def body(t, carry):
            for k in range(TOP_K):
                pltpu.make_async_copy(ys_hbm.at[pl.ds(d_ref[k, t], 1)],
                                      ybuf.at[slot_idx, k, pl.ds(t, 1)], sems.at[slot_idx]).start()
            return carry
        lax.fori_loop(0, tm, body, 0, unroll=8)

    @pl.when(i == 0)
    def _():
        gather(dest_ref, 0)

    @pl.when(i + 1 < n_steps)
    def _():
        gather(dest_next_ref, 1 - slot)

    def drain(t, carry):
        for k in range(TOP_K):
            pltpu.make_async_copy(ys_hbm.at[pl.ds(0, 1)], ybuf.at[slot, k, pl.ds(0, 1)],
                                  sems.at[slot]).wait()
        return carry

    lax.fori_loop(0, tm, drain, 0, unroll=8)

    r = route_ref[...]
    y = r[:, 4:5] * ybuf[slot, 0] + r[:, 5:6] * ybuf[slot, 1]
    mod = mod_ref[0]
    x2 = x1_ref[...] + mod[5:6] * y
    o_ref[...] = _rms(x2, gf_ref[...]) if final_norm else x2


def _combine(dest, ys, x1, route, mod, gf, seq, final_norm):
    t, d = x1.shape
    tm = TM_COMB
    n_steps = t // tm
    per_b = seq // tm
    return pl.pallas_call(
        functools.partial(_combine_kernel, tm=tm, n_steps=n_steps, final_norm=final_norm),
        out_shape=jax.ShapeDtypeStruct((t, d), F32),
        grid=(n_steps,),
        in_specs=[
            pl.BlockSpec((SUBLANES, tm), lambda i: (0, i), memory_space=pltpu.SMEM),
            pl.BlockSpec((SUBLANES, tm), lambda i: (0, jnp.minimum(i + 1, n_steps - 1)),
                         memory_space=pltpu.SMEM),
            pl.BlockSpec(memory_space=pl.ANY),
            pl.BlockSpec((tm, d), lambda i: (i, 0)),
            pl.BlockSpec((tm, LANES), lambda i: (i, 0)),
            pl.BlockSpec((1, N_MOD, d), lambda i: (i // per_b, 0, 0)),
            pl.BlockSpec((1, d), lambda i: (0, 0)),
        ],
        out_specs=pl.BlockSpec((tm, d), lambda i: (i, 0)),
        scratch_shapes=[pltpu.VMEM((2, TOP_K, tm, d), F32), pltpu.SemaphoreType.DMA((2,))],
        compiler_params=pltpu.CompilerParams(
            dimension_semantics=("arbitrary",), vmem_limit_bytes=VMEM_LIMIT),
        name="combine",
    )(dest, dest, ys, x1, route, mod, gf)


def _prep_in_weights(w_in, q_rank, kv_rank, conv_dim):
    o1 = q_rank
    o2 = o1 + kv_rank
    o3 = o2 + D_ROPE
    d = w_in.shape[0]
    w_kr = w_in[:, o2:o3]
    half = D_ROPE // 2
    w_kr_rot = jnp.concatenate([-w_kr[:, half:], w_kr[:, :half]], axis=1)
    zl = jnp.zeros((d, D_NOPE), w_in.dtype)
    zr = jnp.zeros((d, LANES - D_NOPE - D_ROPE), w_in.dtype)
    return jnp.concatenate(
        [w_in[:, :o2], zl, w_kr, zr, zl, w_kr_rot, zr, w_in[:, o3:]], axis=1).astype(BF16)


def _prep_q_weights(w_uq):
    r = w_uq.shape[0]
    w = w_uq.reshape(r, HEADS, D_NOPE + D_ROPE)
    nope, rope = w[..., :D_NOPE], w[..., D_NOPE:]
    half = D_ROPE // 2
    rot = jnp.concatenate([-rope[..., half:], rope[..., :half]], axis=-1)
    zr = jnp.zeros((r, HEADS, HEAD_PAD - D_NOPE - D_ROPE), w_uq.dtype)
    w1 = jnp.concatenate([nope, rope, zr], axis=-1).reshape(r, HEADS * HEAD_PAD)
    w2 = jnp.concatenate([jnp.zeros_like(nope), rot, zr], axis=-1).reshape(r, HEADS * HEAD_PAD)
    return w1.astype(BF16), w2.astype(BF16)


def _prep_k_weights(w_uk):
    r = w_uk.shape[0]
    w = w_uk.reshape(r, HEADS, D_NOPE)
    zr = jnp.zeros((r, HEADS, HEAD_PAD - D_NOPE), w_uk.dtype)
    return jnp.concatenate([w, zr], axis=-1).reshape(r, HEADS * HEAD_PAD).astype(BF16)


def _rope_lane_freqs():
    inv_freq = 1.0 / (ROPE_THETA ** (jnp.arange(0, D_ROPE, 2, dtype=F32) / D_ROPE))
    z = jnp.zeros((D_NOPE,), F32)
    zr = jnp.zeros((LANES - D_NOPE - D_ROPE,), F32)
    return jnp.concatenate([z, inv_freq, inv_freq, zr]).reshape(1, LANES)


def kernel(x, c, positions, w_ada, b_ada, norm1_g, w_in, q_norm_g, kv_norm_g, w_uq, w_uk, w_uv,
           conv_w, conv_b, attn_out_g, conv_out_g, w_out, norm2_g, w_router_group, b_router_group,
           w_router_expert, b_router_expert, w_exp_gate, w_exp_up, w_exp_down, final_g):
    bsz, seq, d = x.shape
    depth = w_ada.shape[0]
    q_rank = q_norm_g.shape[-1]
    kv_rank = kv_norm_g.shape[-1]
    conv_dim = conv_w.shape[-1]
    t = bsz * seq
    assert seq % max(TM_PROJ, TQ, TM_OUT, TM_COMB) == 0 and TQ % TK == 0 and TM_PROJ % TK == 0
    assert CHUNK & (CHUNK - 1) == 0 and TQ % CHUNK == 0
    assert t % max(TM_DEST, TM_DISP, TM_COMB) == 0 and (t * TOP_K) % BLK == 0
    assert w_uq.shape[-1] == HEADS * (D_NOPE + D_ROPE) and w_uv.shape[-1] == HEADS * D_V

    pos = positions.reshape(bsz, seq, 1)
    invf = _rope_lane_freqs()
    row = lambda v: v.reshape(1, -1)
    for l in range(depth):
        mod = _adaln(c, w_ada[l], b_ada[l]).reshape(bsz, N_MOD, d)
        win = _prep_in_weights(w_in[l], q_rank, kv_rank, conv_dim)
        wq1, wq2 = _prep_q_weights(w_uq[l])
        wk = _prep_k_weights(w_uk[l])
        wvt = w_uv[l].T.astype(BF16)
        q, k, vt, conv = _inproj(x, mod, pos, row(norm1_g[l]), win, row(q_norm_g[l]),
                                 row(kv_norm_g[l]), wq1, wq2, wk, wvt, invf, conv_w[l],
                                 row(conv_b[l]), row(conv_out_g[l]))
        attn = _attention(q, k, vt)

        wr = jnp.concatenate(
            [w_router_expert[l], w_router_group[l],
             jnp.zeros((d, LANES - N_EXPERTS - N_GROUPS), F32)], axis=1).astype(BF16)
        br = jnp.concatenate(
            [b_router_expert[l], b_router_group[l],
             jnp.zeros((LANES - N_EXPERTS - N_GROUPS,), F32)]).reshape(1, LANES)
        x1, h2, route, cnt = _outproj(x, attn, conv, mod, row(attn_out_g[l]),
                                      w_out[l].astype(BF16), row(norm2_g[l]), wr, br)
        route = route.reshape(t, LANES)
        dest = _dest(route, cnt)
        xs = _dispatch(dest, h2.reshape(t, d))
        items = _work_items(cnt[0, :N_EXPERTS], t * TOP_K)
        ys = _experts(*items, xs, w_exp_gate[l].astype(BF16), w_exp_up[l].astype(BF16),
                      w_exp_down[l].astype(BF16))
        out = _combine(dest, ys, x1.reshape(t, d), route, mod, row(final_g), seq, l == depth - 1)
        x = out.reshape(bsz, seq, d)
    return x
```

```python
import functools
import math

import jax
import jax.numpy as jnp
from jax import lax
from jax.experimental import pallas as pl
from jax.experimental.pallas import tpu as pltpu

F32 = jnp.float32
BF16 = jnp.bfloat16

CHUNK = 64
HEADS = 8
D_NOPE = 64
D_ROPE = 32
D_V = 64
ROPE_THETA = 10000.0
CONV_WIDTH = 3
N_GROUPS = 4
EXPERTS_PER_GROUP = 8
N_EXPERTS = N_GROUPS * EXPERTS_PER_GROUP
TOP_K = 2
N_MOD = 6
RMS_EPS = 1e-6
NEG_INF = -1e30
LOG2_E = math.log2(math.e)

LANES = 128
SUBLANES = 8
HEAD_PAD = LANES
DEN_ROWS = 16
VMEM_LIMIT = 56 * 1024 * 1024

TM_PROJ = 256
TQ = 512
TK = 256
TM_OUT = 256
TM_DEST = 512
TM_DISP = 512
BLK = 256
TM_COMB = 256
TN_ADA = 512


def _rms(x, g):
    return x * lax.rsqrt(jnp.mean(x * x, axis=-1, keepdims=True) + RMS_EPS) * g


def _dot(a, b):
    return jnp.dot(a, b, preferred_element_type=F32)


def _dot_nt(a, b):
    return lax.dot_general(a, b, (((1,), (1,)), ((), ())), preferred_element_type=F32)


def _split_bf16(x):
    hi = x.astype(BF16)
    lo = (x - hi.astype(F32)).astype(BF16)
    return hi, lo


def _adaln_kernel(c_ref, w_ref, b_ref, o_ref):
    c = c_ref[...]
    ca = c * jax.nn.sigmoid(c)
    c_hi, c_lo = _split_bf16(ca)
    w_hi, w_lo = _split_bf16(w_ref[...])
    acc = _dot(c_hi, w_hi) + _dot(c_hi, w_lo) + _dot(c_lo, w_hi)
    o_ref[...] = acc + b_ref[...]


def _adaln(c, w, b):
    bsz, d = c.shape
    n = w.shape[1]
    return pl.pallas_call(
        _adaln_kernel,
        out_shape=jax.ShapeDtypeStruct((bsz, n), F32),
        grid=(n // TN_ADA,),
        in_specs=[
            pl.BlockSpec((bsz, d), lambda j: (0, 0)),
            pl.BlockSpec((d, TN_ADA), lambda j: (0, j)),
            pl.BlockSpec((1, TN_ADA), lambda j: (0, j)),
        ],
        out_specs=pl.BlockSpec((bsz, TN_ADA), lambda j: (0, j)),
        compiler_params=pltpu.CompilerParams(
            dimension_semantics=("arbitrary",), vmem_limit_bytes=VMEM_LIMIT),
        name="adaln",
    )(c, w, b.reshape(1, n))


def _inproj_kernel(x_ref, mod_ref, pos_ref, g1_ref, win_ref, gq_ref, gkv_ref, wq1_ref, wq2_ref,
                   wk_ref, wvt_ref, invf_ref, cw_ref, cb_ref, gc_ref,
                   q_out, k_out, vt_out, conv_out, zbuf, *, tm, q_rank, kv_rank, conv_dim):
    si = pl.program_id(1)
    x = x_ref[0]
    mod = mod_ref[0]
    h = (_rms(x, g1_ref[...]) * (1.0 + mod[1:2]) + mod[0:1]).astype(BF16)
    proj = _dot(h, win_ref[...])

    o_kv = q_rank
    o_kr = o_kv + kv_rank
    o_u = o_kr + 2 * LANES
    q_c = proj[:, 0:q_rank]
    kv_c = proj[:, o_kv:o_kv + kv_rank]
    kr = proj[:, o_kr:o_kr + LANES]
    kr_rot = proj[:, o_kr + LANES:o_kr + 2 * LANES]
    u = proj[:, o_u:o_u + conv_dim]
    b_gate = proj[:, o_u + conv_dim:o_u + 2 * conv_dim]
    c_gate = proj[:, o_u + 2 * conv_dim:o_u + 3 * conv_dim]

    ang = pos_ref[0].astype(F32) * invf_ref[...]
    cos_t = jnp.cos(ang)
    sin_t = jnp.sin(ang)
    lane = lax.broadcasted_iota(jnp.int32, (1, LANES), 1)
    is_rope = (lane >= D_NOPE) & (lane < D_NOPE + D_ROPE)
    q_scale = LOG2_E / math.sqrt(D_NOPE + D_ROPE)
    q_cos = jnp.where(lane < D_NOPE, q_scale, jnp.where(is_rope, cos_t * q_scale, 0.0))
    q_sin = jnp.where(is_rope, sin_t * q_scale, 0.0)

    qn = _rms(q_c, gq_ref[...]).astype(BF16)
    qa = _dot(qn, wq1_ref[...])
    qb = _dot(qn, wq2_ref[...])
    kvn = _rms(kv_c, gkv_ref[...]).astype(BF16)
    kn = _dot(kvn, wk_ref[...])
    k_rope = kr * cos_t + kr_rot * sin_t
    for hd in range(HEADS):
        sl = slice(hd * HEAD_PAD, (hd + 1) * HEAD_PAD)
        q_out[0, hd] = (qa[:, sl] * q_cos + qb[:, sl] * q_sin).astype(BF16)
        k_out[0, hd] = (kn[:, sl] + k_rope).astype(BF16)

    vt = _dot_nt(wvt_ref[...], kvn)
    for j in range(tm // TK):
        vt_out[0, j] = vt[:, j * TK:(j + 1) * TK].astype(BF16)

    z = c_gate * u

    @pl.when(si == 0)
    def _():
        zbuf[0:SUBLANES, :] = jnp.zeros((SUBLANES, conv_dim), F32)

    zbuf[SUBLANES:SUBLANES + tm, :] = z
    z1 = zbuf[SUBLANES - 1:SUBLANES - 1 + tm, :]
    z2 = zbuf[SUBLANES - 2:SUBLANES - 2 + tm, :]
    cw = cw_ref[...]
    zc = cw[0:1] * z2 + cw[1:2] * z1 + cw[2:3] * z + cb_ref[...]
    conv = b_gate * zc
    conv_out[0] = _rms(conv, gc_ref[...]).astype(BF16)
    zbuf[0:SUBLANES, :] = zbuf[tm:tm + SUBLANES, :]


def _inproj(x, mod, pos, g1, win, gq, gkv, wq1, wq2, wk, wvt, invf, cw, cb, gc):
    bsz, seq, d = x.shape
    tm = TM_PROJ
    q_rank, kv_rank = wq1.shape[0], wk.shape[0]
    conv_dim = cw.shape[1]
    hv = wvt.shape[0]
    n_s = seq // tm
    const2 = lambda b, s: (0, 0)
    kern = functools.partial(_inproj_kernel, tm=tm, q_rank=q_rank, kv_rank=kv_rank,
                             conv_dim=conv_dim)
    return pl.pallas_call(
        kern,
        out_shape=(
            jax.ShapeDtypeStruct((bsz, HEADS, seq, HEAD_PAD), BF16),
            jax.ShapeDtypeStruct((bsz, HEADS, seq, HEAD_PAD), BF16),
            jax.ShapeDtypeStruct((bsz, seq // TK, hv, TK), BF16),
            jax.ShapeDtypeStruct((bsz, seq, conv_dim), BF16),
        ),
        grid=(bsz, n_s),
        in_specs=[
            pl.BlockSpec((1, tm, d), lambda b, s: (b, s, 0)),
            pl.BlockSpec((1, N_MOD, d), lambda b, s: (b, 0, 0)),
            pl.BlockSpec((1, tm, 1), lambda b, s: (b, s, 0)),
            pl.BlockSpec((1, d), const2),
            pl.BlockSpec(win.shape, const2),
            pl.BlockSpec((1, q_rank), const2),
            pl.BlockSpec((1, kv_rank), const2),
            pl.BlockSpec(wq1.shape, const2),
            pl.BlockSpec(wq2.shape, const2),
            pl.BlockSpec(wk.shape, const2),
            pl.BlockSpec(wvt.shape, const2),
            pl.BlockSpec((1, LANES), const2),
            pl.BlockSpec(cw.shape, const2),
            pl.BlockSpec((1, conv_dim), const2),
            pl.BlockSpec((1, conv_dim), const2),
        ],
        out_specs=(
            pl.BlockSpec((1, HEADS, tm, HEAD_PAD), lambda b, s: (b, 0, s, 0)),
            pl.BlockSpec((1, HEADS, tm, HEAD_PAD), lambda b, s: (b, 0, s, 0)),
            pl.BlockSpec((1, tm // TK, hv, TK), lambda b, s: (b, s, 0, 0)),
            pl.BlockSpec((1, tm, conv_dim), lambda b, s: (b, s, 0)),
        ),
        scratch_shapes=[pltpu.VMEM((tm + 2 * SUBLANES, conv_dim), F32)],
        compiler_params=pltpu.CompilerParams(
            dimension_semantics=("arbitrary", "arbitrary"), vmem_limit_bytes=VMEM_LIMIT),
        name="inproj",
    )(x, mod, pos, g1, win, gq, gkv, wq1, wq2, wk, wvt, invf, cw, cb, gc)


def _attn_kernel(q_ref, k_ref, vt_ref, o_ref, *, n_q):
    row = lax.broadcasted_iota(jnp.int32, (TQ, TQ), 0)
    col = lax.broadcasted_iota(jnp.int32, (TQ, TQ), 1)
    shift = CHUNK.bit_length() - 1
    diag_ok = (row >> shift) <= (col >> shift)
    vt_per_step = TQ // TK

    ones_rows = jnp.ones((DEN_ROWS, TQ), BF16)

    def scores(hh, qi, j):
        return _dot_nt(k_ref[0, hh, j * TQ:(j + 1) * TQ, :], q_ref[0, hh, qi * TQ:(qi + 1) * TQ, :])

    def softmax_pv(hh, j, s_t, state, masked):
        m, acc = state
        if masked:
            s_t = jnp.where(diag_ok, s_t, NEG_INF)
        m_new = jnp.maximum(m, jnp.max(s_t, axis=0, keepdims=True))
        p = jnp.exp2(s_t - m_new).astype(BF16)
        alpha = jnp.exp2(m - m_new)
        v_t = jnp.concatenate(
            [vt_ref[0, j * vt_per_step + i, hh * D_V:(hh + 1) * D_V, :]
             for i in range(vt_per_step)], axis=1)
        v_aug = jnp.concatenate([v_t, ones_rows], axis=0)
        acc_new = alpha * acc + _dot(v_aug, p)
        return m_new, acc_new

    steps = [(qi, j) for qi in range(n_q) for j in range(qi + 1)]
    init = (jnp.full((1, TQ), NEG_INF, F32), jnp.zeros((D_V + DEN_ROWS, TQ), F32))
    s_cur = tuple(scores(hh, *steps[0]) for hh in range(2))
    states = None
    for idx, (qi, j) in enumerate(steps):
        s_next = None
        if idx + 1 < len(steps):
            s_next = tuple(scores(hh, *steps[idx + 1]) for hh in range(2))
        if j == 0:
            states = (init, init)
        states = tuple(softmax_pv(hh, j, s_cur[hh], states[hh], j == qi) for hh in range(2))
        if j == qi:
            o_t = jnp.concatenate([acc[:D_V] / acc[D_V:D_V + 1] for (_, acc) in states],
                                  axis=0)
            o_ref[0, qi * TQ:(qi + 1) * TQ, :] = o_t.T.astype(BF16)
        s_cur = s_next


def _attention(q, k, vt):
    bsz, _, seq, _ = q.shape
    n_pair = HEADS // 2
    return pl.pallas_call(
        functools.partial(_attn_kernel, n_q=seq // TQ),
        out_shape=jax.ShapeDtypeStruct((bsz, seq, HEADS * D_V), BF16),
        grid=(bsz, n_pair),
        in_specs=[
            pl.BlockSpec((1, 2, seq, HEAD_PAD), lambda b, p: (b, p, 0, 0)),
            pl.BlockSpec((1, 2, seq, HEAD_PAD), lambda b, p: (b, p, 0, 0)),
            pl.BlockSpec((1, seq // TK, 2 * D_V, TK), lambda b, p: (b, 0, p, 0)),
        ],
        out_specs=pl.BlockSpec((1, seq, 2 * D_V), lambda b, p: (b, 0, p)),
        compiler_params=pltpu.CompilerParams(
            dimension_semantics=("arbitrary", "arbitrary"), vmem_limit_bytes=VMEM_LIMIT),
        name="attn",
    )(q, k, vt)


def _outproj_kernel(x_ref, attn_ref, conv_ref, mod_ref, ga_ref, wout_ref, g2_ref, wr_ref, br_ref,
                    tri_ref, x1_out, h2_out, route_out, cnt_out, base_ref):
    first = (pl.program_id(0) == 0) & (pl.program_id(1) == 0)

    @pl.when(first)
    def _():
        base_ref[...] = jnp.zeros(base_ref.shape, F32)

    mod = mod_ref[0]
    an = _rms(attn_ref[0].astype(F32), ga_ref[...]).astype(BF16)
    mixed = jnp.concatenate([an, conv_ref[0]], axis=-1)
    y = _dot(mixed, wout_ref[...])
    x1 = x_ref[0] + mod[2:3] * y
    x1_out[0] = x1
    h2 = _rms(x1, g2_ref[...]) * (1.0 + mod[4:5]) + mod[3:4]
    h2_out[0] = h2
    logits = _dot(h2.astype(BF16), wr_ref[...]) + br_ref[...]

    tm = logits.shape[0]
    lane = lax.broadcasted_iota(jnp.int32, (tm, LANES), 1).astype(F32)
    big = float(LANES)
    is_grp = (lane >= N_EXPERTS) & (lane < N_EXPERTS + N_GROUPS)
    gl = jnp.where(is_grp, logits, -jnp.inf)
    gmax = jnp.max(gl, axis=-1, keepdims=True)
    gidx = jnp.min(jnp.where(gl == gmax, lane, big), axis=-1, keepdims=True) - N_EXPERTS
    p_group = 1.0 / jnp.sum(jnp.where(is_grp, jnp.exp(logits - gmax), 0.0), axis=-1, keepdims=True)
    lo = gidx * EXPERTS_PER_GROUP
    in_grp = (lane >= lo) & (lane < lo + EXPERTS_PER_GROUP)
    el = jnp.where(in_grp, logits, -jnp.inf)
    v1 = jnp.max(el, axis=-1, keepdims=True)
    i1 = jnp.min(jnp.where(el == v1, lane, big), axis=-1, keepdims=True)
    el2 = jnp.where(lane == i1, -jnp.inf, el)
    v2 = jnp.max(el2, axis=-1, keepdims=True)
    i2 = jnp.min(jnp.where(el2 == v2, lane, big), axis=-1, keepdims=True)
    t = jnp.exp(v2 - v1)
    w1 = p_group / (1.0 + t)
    w2 = p_group * t / (1.0 + t)

    oh1 = lane == i1
    oh2 = lane == i2
    cnt = jnp.where(oh1 | oh2, 1.0, 0.0)
    prefix = _dot(tri_ref[...], cnt.astype(BF16)) + base_ref[0:1, :]
    r1 = jnp.sum(jnp.where(oh1, prefix, 0.0), axis=-1, keepdims=True)
    r2 = jnp.sum(jnp.where(oh2, prefix, 0.0), axis=-1, keepdims=True)
    new_base = base_ref[...] + jnp.sum(cnt, axis=0, keepdims=True)
    base_ref[...] = new_base
    cnt_out[...] = new_base

    route = jnp.where(lane == 0.0, i1, jnp.where(lane == 1.0, i2, jnp.where(
        lane == 2.0, r1, jnp.where(lane == 3.0, r2, jnp.where(
            lane == 4.0, w1, jnp.where(lane == 5.0, w2, 0.0))))))
    route_out[0] = route


def _outproj(x, attn, conv, mod, ga, wout, g2, wr, br):
    bsz, seq, d = x.shape
    tm = TM_OUT
    aw = attn.shape[-1]
    cwd = conv.shape[-1]
    tri = (lax.broadcasted_iota(jnp.int32, (tm, tm), 1)
           < lax.broadcasted_iota(jnp.int32, (tm, tm), 0)).astype(BF16)
    const2 = lambda b, s: (0, 0)
    tile3 = lambda b, s: (b, s, 0)
    return pl.pallas_call(
        _outproj_kernel,
        out_shape=(
            jax.ShapeDtypeStruct((bsz, seq, d), F32),
            jax.ShapeDtypeStruct((bsz, seq, d), F32),
            jax.ShapeDtypeStruct((bsz, seq, LANES), F32),
            jax.ShapeDtypeStruct((SUBLANES, LANES), F32),
        ),
        grid=(bsz, seq // tm),
        in_specs=[
            pl.BlockSpec((1, tm, d), tile3),
            pl.BlockSpec((1, tm, aw), tile3),
            pl.BlockSpec((1, tm, cwd), tile3),
            pl.BlockSpec((1, N_MOD, d), lambda b, s: (b, 0, 0)),
            pl.BlockSpec((1, aw), const2),
            pl.BlockSpec(wout.shape, const2),
            pl.BlockSpec((1, d), const2),
            pl.BlockSpec(wr.shape, const2),
            pl.BlockSpec((1, LANES), const2),
            pl.BlockSpec((tm, tm), const2),
        ],
        out_specs=(
            pl.BlockSpec((1, tm, d), tile3),
            pl.BlockSpec((1, tm, d), tile3),
            pl.BlockSpec((1, tm, LANES), tile3),
            pl.BlockSpec((SUBLANES, LANES), const2),
        ),
        scratch_shapes=[pltpu.VMEM((SUBLANES, LANES), F32)],
        compiler_params=pltpu.CompilerParams(
            dimension_semantics=("arbitrary", "arbitrary"), vmem_limit_bytes=VMEM_LIMIT),
        name="outproj",
    )(x, attn, conv, mod, ga, wout, g2, wr, br, tri)


def _dest_kernel(route_ref, cnt_ref, dest_out):
    cnt = cnt_ref[...]
    lane8 = lax.broadcasted_iota(jnp.int32, (SUBLANES, LANES), 1)
    inc = cnt
    shift = 1
    while shift < N_EXPERTS:
        inc = inc + jnp.where(lane8 >= shift, pltpu.roll(inc, shift, axis=1), 0.0)
        shift *= 2
    starts = (inc - cnt)[0:1, :]
    r = route_ref[...]
    tm = r.shape[0]
    lane = lax.broadcasted_iota(jnp.int32, (tm, LANES), 1).astype(F32)
    d1 = r[:, 2:3] + jnp.sum(jnp.where(lane == r[:, 0:1], starts, 0.0), axis=-1, keepdims=True)
    d2 = r[:, 3:4] + jnp.sum(jnp.where(lane == r[:, 1:2], starts, 0.0), axis=-1, keepdims=True)
    packed = jnp.where(lane == 0.0, d1, jnp.where(lane == 1.0, d2, 0.0))
    dest_out[...] = packed.T[0:SUBLANES, :].astype(jnp.int32)


def _dest(route, cnt):
    t = route.shape[0]
    tm = TM_DEST
    return pl.pallas_call(
        _dest_kernel,
        out_shape=jax.ShapeDtypeStruct((SUBLANES, t), jnp.int32),
        grid=(t // tm,),
        in_specs=[
            pl.BlockSpec((tm, LANES), lambda i: (i, 0)),
            pl.BlockSpec((SUBLANES, LANES), lambda i: (0, 0)),
        ],
        out_specs=pl.BlockSpec((SUBLANES, tm), lambda i: (0, i)),
        compiler_params=pltpu.CompilerParams(
            dimension_semantics=("arbitrary",), vmem_limit_bytes=VMEM_LIMIT),
        name="dest",
    )(route, cnt)


def _row_copy(src, src_row, dst, dst_row, sem):
    return pltpu.make_async_copy(src.at[pl.ds(src_row, 1)], dst.at[pl.ds(dst_row, 1)], sem)


def _dispatch_kernel(dest_ref, h_ref, xs_hbm, sem, *, tm):
    def issue(t, carry):
        for k in range(TOP_K):
            _row_copy(h_ref, t, xs_hbm, dest_ref[k, t], sem).start()
        return carry

    lax.fori_loop(0, tm, issue, 0, unroll=8)

    def drain(t, carry):
        for k in range(TOP_K):
            _row_copy(h_ref, 0, xs_hbm, 0, sem).wait()
        return carry

    lax.fori_loop(0, tm, drain, 0, unroll=8)


def _dispatch(dest, h):
    t, d = h.shape
    tm = TM_DISP
    return pl.pallas_call(
        functools.partial(_dispatch_kernel, tm=tm),
        out_shape=jax.ShapeDtypeStruct((t * TOP_K, d), h.dtype),
        grid=(t // tm,),
        in_specs=[
            pl.BlockSpec((SUBLANES, tm), lambda i: (0, i), memory_space=pltpu.SMEM),
            pl.BlockSpec((tm, d), lambda i: (i, 0)),
        ],
        out_specs=pl.BlockSpec(memory_space=pl.ANY),
        scratch_shapes=[pltpu.SemaphoreType.DMA],
        compiler_params=pltpu.CompilerParams(
            dimension_semantics=("arbitrary",), vmem_limit_bytes=VMEM_LIMIT),
        name="dispatch",
    )(dest, h)


def _experts_kernel(blk_ref, exp_ref, lo_ref, hi_ref, x_ref, wg_ref, wu_ref, wd_ref, o_ref):
    i = pl.program_id(0)
    lo = lo_ref[i]
    hi = hi_ref[i]
    prev = jnp.maximum(i - 1, 0)
    first = (i == 0) | (blk_ref[i] != blk_ref[prev])

    @pl.when(first)
    def _():
        o_ref[...] = jnp.zeros(o_ref.shape, o_ref.dtype)

    @pl.when(hi > lo)
    def _():
        x = x_ref[...].astype(BF16)
        g = _dot(x, wg_ref[0])
        u = _dot(x, wu_ref[0])
        a = (g * jax.nn.sigmoid(g) * u).astype(BF16)
        y = _dot(a, wd_ref[0])
        row = lax.broadcasted_iota(jnp.int32, (x.shape[0], 1), 0)
        keep = (row >= lo) & (row < hi)
        o_ref[...] += jnp.where(keep, y, 0.0)


def _experts(item_blk, item_exp, item_lo, item_hi, xs, wg, wu, wd):
    a, d = xs.shape
    de = wg.shape[-1]
    n_items = item_blk.shape[0]
    grid_spec = pltpu.PrefetchScalarGridSpec(
        num_scalar_prefetch=4,
        grid=(n_items,),
        in_specs=[
            pl.BlockSpec((BLK, d), lambda i, blk, ex, lo, hi: (blk[i], 0)),
            pl.BlockSpec((1, d, de), lambda i, blk, ex, lo, hi: (ex[i], 0, 0)),
            pl.BlockSpec((1, d, de), lambda i, blk, ex, lo, hi: (ex[i], 0, 0)),
            pl.BlockSpec((1, de, d), lambda i, blk, ex, lo, hi: (ex[i], 0, 0)),
        ],
        out_specs=pl.BlockSpec((BLK, d), lambda i, blk, ex, lo, hi: (blk[i], 0)),
    )
    return pl.pallas_call(
        _experts_kernel,
        out_shape=jax.ShapeDtypeStruct((a, d), F32),
        grid_spec=grid_spec,
        compiler_params=pltpu.CompilerParams(
            dimension_semantics=("arbitrary",), vmem_limit_bytes=VMEM_LIMIT),
        name="experts",
    )(item_blk, item_exp, item_lo, item_hi, xs, wg, wu, wd)


def _work_items(counts, n_rows):
    n_blocks = n_rows // BLK
    n_items = n_blocks + N_EXPERTS - 1
    counts = counts.astype(jnp.int32)
    ends = jnp.cumsum(counts)
    starts = ends - counts
    first_blk = starts // BLK
    last_blk = jnp.maximum(ends - 1, 0) // BLK
    n_e = jnp.where(counts > 0, last_blk - first_blk + 1, 0)
    item_end = jnp.cumsum(n_e)
    item_start = item_end - n_e
    idx = jnp.arange(n_items, dtype=jnp.int32)
    total = item_end[-1]
    ex = jnp.minimum(jnp.sum((idx[:, None] >= item_end[None, :]).astype(jnp.int32), axis=1),
                     N_EXPERTS - 1)
    onehot = (ex[:, None] == jnp.arange(N_EXPERTS, dtype=jnp.int32)[None, :]).astype(jnp.int32)
    pick = lambda table: jnp.sum(onehot * table[None, :], axis=1)
    blk = pick(first_blk) + (idx - pick(item_start))
    lo = jnp.maximum(pick(starts), blk * BLK) - blk * BLK
    hi = jnp.minimum(pick(ends), (blk + 1) * BLK) - blk * BLK
    valid = idx < total
    last_ex = jnp.max(jnp.where(valid, ex, 0))
    blk = jnp.where(valid, blk, n_blocks - 1).astype(jnp.int32)
    ex = jnp.where(valid, ex, last_ex).astype(jnp.int32)
    lo = jnp.where(valid, lo, 0).astype(jnp.int32)
    hi = jnp.where(valid, hi, 0).astype(jnp.int32)
    return blk, ex, lo, hi


def _combine_kernel(dest_ref, dest_next_ref, ys_hbm, x1_ref, route_ref, mod_ref, gf_ref, o_ref,
                    ybuf, sems, *, tm, n_steps, final_norm):
    i = pl.program_id(0)
    slot = i % 2

    def gather(d_ref, slot_idx):
        def body(t, carry):
            for k in range(TOP_K):
                pltpu.make_async_copy(ys_hbm.at[pl.ds(d_ref[k, t], 1)],
                                      ybuf.at[slot_idx, k, pl.ds(t, 1)], sems.at[slot_idx]).start()
            return carry
        lax.fori_loop(0, tm, body, 0, unroll=8)

    @pl.when(i == 0)
    def _():
        gather(dest_ref, 0)

    @pl.when(i + 1 < n_steps)
    def _():
        gather(dest_next_ref, 1 - slot)

    def drain(t, carry):
        for k in range(TOP_K):
            pltpu.make_async_copy(ys_hbm.at[pl.ds(0, 1)], ybuf.at[slot, k, pl.ds(0, 1)],
                                  sems.at[slot]).wait()
        return carry

    lax.fori_loop(0, tm, drain, 0, unroll=8)

    r = route_ref[...]
    y = r[:, 4:5] * ybuf[slot, 0] + r[:, 5:6] * ybuf[slot, 1]
    mod = mod_ref[0]
    x2 = x1_ref[...] + mod[5:6] * y
    o_ref[...] = _rms(x2, gf_ref[...]) if final_norm else x2


def _combine(dest, ys, x1, route, mod, gf, seq, final_norm):
    t, d = x1.shape
    tm = TM_COMB
    n_steps = t // tm
    per_b = seq // tm
    return pl.pallas_call(
        functools.partial(_combine_kernel, tm=tm, n_steps=n_steps, final_norm=final_norm),
        out_shape=jax.ShapeDtypeStruct((t, d), F32),
        grid=(n_steps,),
        in_specs=[
            pl.BlockSpec((SUBLANES, tm), lambda i: (0, i), memory_space=pltpu.SMEM),
            pl.BlockSpec((SUBLANES, tm), lambda i: (0, jnp.minimum(i + 1, n_steps - 1)),
                         memory_space=pltpu.SMEM),
            pl.BlockSpec(memory_space=pl.ANY),
            pl.BlockSpec((tm, d), lambda i: (i, 0)),
            pl.BlockSpec((tm, LANES), lambda i: (i, 0)),
            pl.BlockSpec((1, N_MOD, d), lambda i: (i // per_b, 0, 0)),
            pl.BlockSpec((1, d), lambda i: (0, 0)),
        ],
        out_specs=pl.BlockSpec((tm, d), lambda i: (i, 0)),
        scratch_shapes=[pltpu.VMEM((2, TOP_K, tm, d), F32), pltpu.SemaphoreType.DMA((2,))],
        compiler_params=pltpu.CompilerParams(
            dimension_semantics=("arbitrary",), vmem_limit_bytes=VMEM_LIMIT),
        name="combine",
    )(dest, dest, ys, x1, route, mod, gf)


def _prep_in_weights(w_in, q_rank, kv_rank, conv_dim):
    o1 = q_rank
    o2 = o1 + kv_rank
    o3 = o2 + D_ROPE
    d = w_in.shape[0]
    w_kr = w_in[:, o2:o3]
    half = D_ROPE // 2
    w_kr_rot = jnp.concatenate([-w_kr[:, half:], w_kr[:, :half]], axis=1)
    zl = jnp.zeros((d, D_NOPE), w_in.dtype)
    zr = jnp.zeros((d, LANES - D_NOPE - D_ROPE), w_in.dtype)
    return jnp.concatenate(
        [w_in[:, :o2], zl, w_kr, zr, zl, w_kr_rot, zr, w_in[:, o3:]], axis=1).astype(BF16)


def _prep_q_weights(w_uq):
    r = w_uq.shape[0]
    w = w_uq.reshape(r, HEADS, D_NOPE + D_ROPE)
    nope, rope = w[..., :D_NOPE], w[..., D_NOPE:]
    half = D_ROPE // 2
    rot = jnp.concatenate([-rope[..., half:], rope[..., :half]], axis=-1)
    zr = jnp.zeros((r, HEADS, HEAD_PAD - D_NOPE - D_ROPE), w_uq.dtype)
    w1 = jnp.concatenate([nope, rope, zr], axis=-1).reshape(r, HEADS * HEAD_PAD)
    w2 = jnp.concatenate([jnp.zeros_like(nope), rot, zr], axis=-1).reshape(r, HEADS * HEAD_PAD)
    return w1.astype(BF16), w2.astype(BF16)


def _prep_k_weights(w_uk):
    r = w_uk.shape[0]
    w = w_uk.reshape(r, HEADS, D_NOPE)
    zr = jnp.zeros((r, HEADS, HEAD_PAD - D_NOPE), w_uk.dtype)
    return jnp.concatenate([w, zr], axis=-1).reshape(r, HEADS * HEAD_PAD).astype(BF16)


def _rope_lane_freqs():
    inv_freq = 1.0 / (ROPE_THETA ** (jnp.arange(0, D_ROPE, 2, dtype=F32) / D_ROPE))
    z = jnp.zeros((D_NOPE,), F32)
    zr = jnp.zeros((LANES - D_NOPE - D_ROPE,), F32)
    return jnp.concatenate([z, inv_freq, inv_freq, zr]).reshape(1, LANES)


def kernel(x, c, positions, w_ada, b_ada, norm1_g, w_in, q_norm_g, kv_norm_g, w_uq, w_uk, w_uv,
           conv_w, conv_b, attn_out_g, conv_out_g, w_out, norm2_g, w_router_group, b_router_group,
           w_router_expert, b_router_expert, w_exp_gate, w_exp_up, w_exp_down, final_g):
    bsz, seq, d = x.shape
    depth = w_ada.shape[0]
    q_rank = q_norm_g.shape[-1]
    kv_rank = kv_norm_g.shape[-1]
    conv_dim = conv_w.shape[-1]
    t = bsz * seq
    assert seq % max(TM_PROJ, TQ, TM_OUT, TM_COMB) == 0 and TQ % TK == 0 and TM_PROJ % TK == 0
    assert CHUNK & (CHUNK - 1) == 0 and TQ % CHUNK == 0
    assert t % max(TM_DEST, TM_DISP, TM_COMB) == 0 and (t * TOP_K) % BLK == 0
    assert w_uq.shape[-1] == HEADS * (D_NOPE + D_ROPE) and w_uv.shape[-1] == HEADS * D_V

    pos = positions.reshape(bsz, seq, 1)
    invf = _rope_lane_freqs()
    row = lambda v: v.reshape(1, -1)
    for l in range(depth):
        mod = _adaln(c, w_ada[l], b_ada[l]).reshape(bsz, N_MOD, d)
        win = _prep_in_weights(w_in[l], q_rank, kv_rank, conv_dim)
        wq1, wq2 = _prep_q_weights(w_uq[l])
        wk = _prep_k_weights(w_uk[l])
        wvt = w_uv[l].T.astype(BF16)
        q, k, vt, conv = _inproj(x, mod, pos, row(norm1_g[l]), win, row(q_norm_g[l]),
                                 row(kv_norm_g[l]), wq1, wq2, wk, wvt, invf, conv_w[l],
                                 row(conv_b[l]), row(conv_out_g[l]))
        attn = _attention(q, k, vt)

        wr = jnp.concatenate(
            [w_router_expert[l], w_router_group[l],
             jnp.zeros((d, LANES - N_EXPERTS - N_GROUPS), F32)], axis=1).astype(BF16)
        br = jnp.concatenate(
            [b_router_expert[l], b_router_group[l],
             jnp.zeros((LANES - N_EXPERTS - N_GROUPS,), F32)]).reshape(1, LANES)
        x1, h2, route, cnt = _outproj(x, attn, conv, mod, row(attn_out_g[l]),
                                      w_out[l].astype(BF16), row(norm2_g[l]), wr, br)
        route = route.reshape(t, LANES)
        dest = _dest(route, cnt)
        xs = _dispatch(dest, h2.reshape(t, d))
        items = _work_items(cnt[0, :N_EXPERTS], t * TOP_K)
        ys = _experts(*items, xs, w_exp_gate[l].astype(BF16), w_exp_up[l].astype(BF16),
                      w_exp_down[l].astype(BF16))
        out = _combine(dest, ys, x1.reshape(t, d), route, mod, row(final_g), seq, l == depth - 1)
        x = out.reshape(bsz, seq, d)
    return x
```

```python
import functools
import math

import jax
import jax.numpy as jnp
from jax import lax
from jax.experimental import pallas as pl
from jax.experimental.pallas import tpu as pltpu

F32 = jnp.float32
BF16 = jnp.bfloat16

CHUNK = 64
HEADS = 8
D_NOPE = 64
D_ROPE = 32
D_V = 64
ROPE_THETA = 10000.0
CONV_WIDTH = 3
N_GROUPS = 4
EXPERTS_PER_GROUP = 8
N_EXPERTS = N_GROUPS * EXPERTS_PER_GROUP
TOP_K = 2
N_MOD = 6
RMS_EPS = 1e-6
NEG_INF = -1e30
LOG2_E = math.log2(math.e)

LANES = 128
SUBLANES = 8
HEAD_PAD = LANES
DEN_ROWS = 16
VMEM_LIMIT = 56 * 1024 * 1024

TM_PROJ = 512
TQ = 512
TK = 256
TM_OUT = 256
TM_DEST = 512
TM_DISP = 512
BLK = 256
TM_COMB = 256
TN_ADA = 512


def _rms(x, g):
    return x * lax.rsqrt(jnp.mean(x * x, axis=-1, keepdims=True) + RMS_EPS) * g


def _dot(a, b):
    return jnp.dot(a, b, preferred_element_type=F32)


def _dot_nt(a, b):
    return lax.dot_general(a, b, (((1,), (1,)), ((), ())), preferred_element_type=F32)


def _store_token_rows(ref, x):
    n, d = x.shape
    r = d // LANES
    for j in range(r):
        ref[pl.ds(j, n, stride=r), :] = x[:, j * LANES:(j + 1) * LANES]


def _load_token_rows(ref, n, r):
    return jnp.concatenate([ref[pl.ds(j, n, stride=r), :] for j in range(r)], axis=1)


def _split_bf16(x):
    hi = x.astype(BF16)
    lo = (x - hi.astype(F32)).astype(BF16)
    return hi, lo


def _adaln_kernel(c_ref, w_ref, b_ref, o_ref):
    c = c_ref[...]
    ca = c * jax.nn.sigmoid(c)
    c_hi, c_lo = _split_bf16(ca)
    w_hi, w_lo = _split_bf16(w_ref[...])
    acc = _dot(c_hi, w_hi) + _dot(c_hi, w_lo) + _dot(c_lo, w_hi)
    o_ref[...] = acc + b_ref[...]


def _adaln(c, w, b):
    bsz, d = c.shape
    n = w.shape[1]
    return pl.pallas_call(
        _adaln_kernel,
        out_shape=jax.ShapeDtypeStruct((bsz, n), F32),
        grid=(n // TN_ADA,),
        in_specs=[
            pl.BlockSpec((bsz, d), lambda j: (0, 0)),
            pl.BlockSpec((d, TN_ADA), lambda j: (0, j)),
            pl.BlockSpec((1, TN_ADA), lambda j: (0, j)),
        ],
        out_specs=pl.BlockSpec((bsz, TN_ADA), lambda j: (0, j)),
        compiler_params=pltpu.CompilerParams(
            dimension_semantics=("arbitrary",), vmem_limit_bytes=VMEM_LIMIT),
        name="adaln",
    )(c, w, b.reshape(1, n))


def _inproj_kernel(x_ref, mod_ref, pos_ref, g1_ref, win_ref, gq_ref, gkv_ref, wq1_ref, wq2_ref,
                   wk_ref, wvt_ref, invf_ref, cw_ref, cb_ref, gc_ref,
                   q_out, k_out, vt_out, conv_out, zbuf, *, tm, q_rank, kv_rank, conv_dim):
    si = pl.program_id(1)
    x = x_ref[0]
    mod = mod_ref[0]
    h = (_rms(x, g1_ref[...]) * (1.0 + mod[1:2]) + mod[0:1]).astype(BF16)
    proj = _dot(h, win_ref[...])

    o_kv = q_rank
    o_kr = o_kv + kv_rank
    o_u = o_kr + 2 * LANES
    q_c = proj[:, 0:q_rank]
    kv_c = proj[:, o_kv:o_kv + kv_rank]
    kr = proj[:, o_kr:o_kr + LANES]
    kr_rot = proj[:, o_kr + LANES:o_kr + 2 * LANES]
    u = proj[:, o_u:o_u + conv_dim]
    b_gate = proj[:, o_u + conv_dim:o_u + 2 * conv_dim]
    c_gate = proj[:, o_u + 2 * conv_dim:o_u + 3 * conv_dim]

    ang_t = invf_ref[...] * pos_ref[0].astype(F32)

    def head_lanes(tab_t):
        zl = jnp.zeros((D_NOPE, tm), F32)
        zr = jnp.zeros((LANES - D_NOPE - D_ROPE, tm), F32)
        return jnp.concatenate([zl, tab_t, tab_t, zr], axis=0).T

    cos_t = head_lanes(jnp.cos(ang_t))
    sin_t = head_lanes(jnp.sin(ang_t))
    lane = lax.broadcasted_iota(jnp.int32, (1, LANES), 1)
    q_scale = LOG2_E / math.sqrt(D_NOPE + D_ROPE)
    q_cos = jnp.where(lane < D_NOPE, q_scale, cos_t * q_scale)
    q_sin = sin_t * q_scale

    qn = _rms(q_c, gq_ref[...]).astype(BF16)
    qa = _dot(qn, wq1_ref[...])
    qb = _dot(qn, wq2_ref[...])
    kvn = _rms(kv_c, gkv_ref[...]).astype(BF16)
    kn = _dot(kvn, wk_ref[...])
    k_rope = kr * cos_t + kr_rot * sin_t
    for hd in range(HEADS):
        sl = slice(hd * HEAD_PAD, (hd + 1) * HEAD_PAD)
        q_out[0, hd] = (qa[:, sl] * q_cos + qb[:, sl] * q_sin).astype(BF16)
        k_out[0, hd] = (kn[:, sl] + k_rope).astype(BF16)

    vt = _dot_nt(wvt_ref[...], kvn)
    for j in range(tm // TK):
        vt_out[0, j] = vt[:, j * TK:(j + 1) * TK].astype(BF16)

    z = c_gate * u

    @pl.when(si == 0)
    def _():
        zbuf[0:SUBLANES, :] = jnp.zeros((SUBLANES, conv_dim), F32)

    zbuf[SUBLANES:SUBLANES + tm, :] = z
    z1 = zbuf[SUBLANES - 1:SUBLANES - 1 + tm, :]
    z2 = zbuf[SUBLANES - 2:SUBLANES - 2 + tm, :]
    cw = cw_ref[...]
    zc = cw[0:1] * z2 + cw[1:2] * z1 + cw[2:3] * z + cb_ref[...]
    conv = b_gate * zc
    conv_out[0] = _rms(conv, gc_ref[...]).astype(BF16)
    zbuf[0:SUBLANES, :] = zbuf[tm:tm + SUBLANES, :]


def _inproj(x, mod, pos, g1, win, gq, gkv, wq1, wq2, wk, wvt, invf, cw, cb, gc):
    bsz, seq, d = x.shape
    tm = TM_PROJ
    q_rank, kv_rank = wq1.shape[0], wk.shape[0]
    conv_dim = cw.shape[1]
    hv = wvt.shape[0]
    n_s = seq // tm
    const2 = lambda b, s: (0, 0)
    kern = functools.partial(_inproj_kernel, tm=tm, q_rank=q_rank, kv_rank=kv_rank,
                             conv_dim=conv_dim)
    return pl.pallas_call(
        kern,
        out_shape=(
            jax.ShapeDtypeStruct((bsz, HEADS, seq, HEAD_PAD), BF16),
            jax.ShapeDtypeStruct((bsz, HEADS, seq, HEAD_PAD), BF16),
            jax.ShapeDtypeStruct((bsz, seq // TK, hv, TK), BF16),
            jax.ShapeDtypeStruct((bsz, seq, conv_dim), BF16),
        ),
        grid=(bsz, n_s),
        in_specs=[
            pl.BlockSpec((1, tm, d), lambda b, s: (b, s, 0)),
            pl.BlockSpec((1, N_MOD, d), lambda b, s: (b, 0, 0)),
            pl.BlockSpec((1, 1, tm), lambda b, s: (b, 0, s)),
            pl.BlockSpec((1, d), const2),
            pl.BlockSpec(win.shape, const2),
            pl.BlockSpec((1, q_rank), const2),
            pl.BlockSpec((1, kv_rank), const2),
            pl.BlockSpec(wq1.shape, const2),
            pl.BlockSpec(wq2.shape, const2),
            pl.BlockSpec(wk.shape, const2),
            pl.BlockSpec(wvt.shape, const2),
            pl.BlockSpec(invf.shape, const2),
            pl.BlockSpec(cw.shape, const2),
            pl.BlockSpec((1, conv_dim), const2),
            pl.BlockSpec((1, conv_dim), const2),
        ],
        out_specs=(
            pl.BlockSpec((1, HEADS, tm, HEAD_PAD), lambda b, s: (b, 0, s, 0)),
            pl.BlockSpec((1, HEADS, tm, HEAD_PAD), lambda b, s: (b, 0, s, 0)),
            pl.BlockSpec((1, tm // TK, hv, TK), lambda b, s: (b, s, 0, 0)),
            pl.BlockSpec((1, tm, conv_dim), lambda b, s: (b, s, 0)),
        ),
        scratch_shapes=[pltpu.VMEM((tm + 2 * SUBLANES, conv_dim), F32)],
        compiler_params=pltpu.CompilerParams(
            dimension_semantics=("arbitrary", "arbitrary"), vmem_limit_bytes=VMEM_LIMIT),
        name="inproj",
    )(x, mod, pos, g1, win, gq, gkv, wq1, wq2, wk, wvt, invf, cw, cb, gc)


def _attn_kernel(q_ref, k_ref, vt_ref, o_ref, *, n_q):
    row = lax.broadcasted_iota(jnp.int32, (TQ, TQ), 0)
    col = lax.broadcasted_iota(jnp.int32, (TQ, TQ), 1)
    shift = CHUNK.bit_length() - 1
    diag_ok = (row >> shift) <= (col >> shift)
    vt_per_step = TQ // TK

    ones_rows = jnp.ones((DEN_ROWS, TQ), BF16)

    def scores(hh, qi, j):
        return _dot_nt(k_ref[0, hh, j * TQ:(j + 1) * TQ, :], q_ref[0, hh, qi * TQ:(qi + 1) * TQ, :])

    def softmax_pv(hh, j, s_t, state, masked):
        m, acc = state
        if masked:
            s_t = jnp.where(diag_ok, s_t, NEG_INF)
        m_new = jnp.maximum(m, jnp.max(s_t, axis=0, keepdims=True))
        p = jnp.exp2(s_t - m_new).astype(BF16)
        alpha = jnp.exp2(m - m_new)
        v_t = jnp.concatenate(
            [vt_ref[0, j * vt_per_step + i, hh * D_V:(hh + 1) * D_V, :]
             for i in range(vt_per_step)], axis=1)
        v_aug = jnp.concatenate([v_t, ones_rows], axis=0)
        acc_new = alpha * acc + _dot(v_aug, p)
        return m_new, acc_new

    steps = [(qi, j) for qi in range(n_q) for j in range(qi + 1)]
    init = (jnp.full((1, TQ), NEG_INF, F32), jnp.zeros((D_V + DEN_ROWS, TQ), F32))
    s_cur = tuple(scores(hh, *steps[0]) for hh in range(2))
    states = None
    for idx, (qi, j) in enumerate(steps):
        s_next = None
        if idx + 1 < len(steps):
            s_next = tuple(scores(hh, *steps[idx + 1]) for hh in range(2))
        if j == 0:
            states = (init, init)
        states = tuple(softmax_pv(hh, j, s_cur[hh], states[hh], j == qi) for hh in range(2))
        if j == qi:
            o_t = jnp.concatenate([acc[:D_V] / acc[D_V:D_V + 1] for (_, acc) in states],
                                  axis=0)
            o_ref[0, qi * TQ:(qi + 1) * TQ, :] = o_t.T.astype(BF16)
        s_cur = s_next


def _attention(q, k, vt):
    bsz, _, seq, _ = q.shape
    n_pair = HEADS // 2
    return pl.pallas_call(
        functools.partial(_attn_kernel, n_q=seq // TQ),
        out_shape=jax.ShapeDtypeStruct((bsz, seq, HEADS * D_V), BF16),
        grid=(bsz, n_pair),
        in_specs=[
            pl.BlockSpec((1, 2, seq, HEAD_PAD), lambda b, p: (b, p, 0, 0)),
            pl.BlockSpec((1, 2, seq, HEAD_PAD), lambda b, p: (b, p, 0, 0)),
            pl.BlockSpec((1, seq // TK, 2 * D_V, TK), lambda b, p: (b, 0, p, 0)),
        ],
        out_specs=pl.BlockSpec((1, seq, 2 * D_V), lambda b, p: (b, 0, p)),
        compiler_params=pltpu.CompilerParams(
            dimension_semantics=("arbitrary", "arbitrary"), vmem_limit_bytes=VMEM_LIMIT),
        name="attn",
    )(q, k, vt)


def _outproj_kernel(x_ref, attn_ref, conv_ref, mod_ref, ga_ref, wout_ref, g2_ref, wr_ref, br_ref,
                    tri_ref, x1_out, h2_out, route_out, cnt_out, base_ref):
    first = (pl.program_id(0) == 0) & (pl.program_id(1) == 0)

    @pl.when(first)
    def _():
        base_ref[...] = jnp.zeros(base_ref.shape, F32)

    mod = mod_ref[0]
    an = _rms(attn_ref[0].astype(F32), ga_ref[...]).astype(BF16)
    mixed = jnp.concatenate([an, conv_ref[0]], axis=-1)
    y = _dot(mixed, wout_ref[...])
    x1 = x_ref[0] + mod[2:3] * y
    x1_out[0] = x1
    h2 = _rms(x1, g2_ref[...]) * (1.0 + mod[4:5]) + mod[3:4]
    _store_token_rows(h2_out, h2)
    logits = _dot(h2.astype(BF16), wr_ref[...]) + br_ref[...]

    tm = logits.shape[0]
    lane = lax.broadcasted_iota(jnp.int32, (tm, LANES), 1).astype(F32)
    big = float(LANES)
    is_grp = (lane >= N_EXPERTS) & (lane < N_EXPERTS + N_GROUPS)
    gl = jnp.where(is_grp, logits, -jnp.inf)
    gmax = jnp.max(gl, axis=-1, keepdims=True)
    gidx = jnp.min(jnp.where(gl == gmax, lane, big), axis=-1, keepdims=True) - N_EXPERTS
    p_group = 1.0 / jnp.sum(jnp.where(is_grp, jnp.exp(logits - gmax), 0.0), axis=-1, keepdims=True)
    lo = gidx * EXPERTS_PER_GROUP
    in_grp = (lane >= lo) & (lane < lo + EXPERTS_PER_GROUP)
    el = jnp.where(in_grp, logits, -jnp.inf)
    v1 = jnp.max(el, axis=-1, keepdims=True)
    i1 = jnp.min(jnp.where(el == v1, lane, big), axis=-1, keepdims=True)
    el2 = jnp.where(lane == i1, -jnp.inf, el)
    v2 = jnp.max(el2, axis=-1, keepdims=True)
    i2 = jnp.min(jnp.where(el2 == v2, lane, big), axis=-1, keepdims=True)
    t = jnp.exp(v2 - v1)
    w1 = p_group / (1.0 + t)
    w2 = p_group * t / (1.0 + t)

    oh1 = lane == i1
    oh2 = lane == i2
    cnt = jnp.where(oh1 | oh2, 1.0, 0.0)
    prefix = _dot(tri_ref[...], cnt.astype(BF16)) + base_ref[0:1, :]
    r1 = jnp.sum(jnp.where(oh1, prefix, 0.0), axis=-1, keepdims=True)
    r2 = jnp.sum(jnp.where(oh2, prefix, 0.0), axis=-1, keepdims=True)
    new_base = base_ref[...] + jnp.sum(cnt, axis=0, keepdims=True)
    base_ref[...] = new_base
    cnt_out[...] = new_base

    route = jnp.where(lane == 0.0, i1, jnp.where(lane == 1.0, i2, jnp.where(
        lane == 2.0, r1, jnp.where(lane == 3.0, r2, jnp.where(
            lane == 4.0, w1, jnp.where(lane == 5.0, w2, 0.0))))))
    route_out[0] = route


def _outproj(x, attn, conv, mod, ga, wout, g2, wr, br):
    bsz, seq, d = x.shape
    tm = TM_OUT
    aw = attn.shape[-1]
    cwd = conv.shape[-1]
    n_s = seq // tm
    rpt = d // LANES
    tri = (lax.broadcasted_iota(jnp.int32, (tm, tm), 1)
           < lax.broadcasted_iota(jnp.int32, (tm, tm), 0)).astype(BF16)
    const2 = lambda b, s: (0, 0)
    tile3 = lambda b, s: (b, s, 0)
    return pl.pallas_call(
        _outproj_kernel,
        out_shape=(
            jax.ShapeDtypeStruct((bsz, seq, d), F32),
            jax.ShapeDtypeStruct((bsz * seq * rpt, LANES), F32),
            jax.ShapeDtypeStruct((bsz, seq, LANES), F32),
            jax.ShapeDtypeStruct((SUBLANES, LANES), F32),
        ),
        grid=(bsz, n_s),
        in_specs=[
            pl.BlockSpec((1, tm, d), tile3),
            pl.BlockSpec((1, tm, aw), tile3),
            pl.BlockSpec((1, tm, cwd), tile3),
            pl.BlockSpec((1, N_MOD, d), lambda b, s: (b, 0, 0)),
            pl.BlockSpec((1, aw), const2),
            pl.BlockSpec(wout.shape, const2),
            pl.BlockSpec((1, d), const2),
            pl.BlockSpec(wr.shape, const2),
            pl.BlockSpec((1, LANES), const2),
            pl.BlockSpec((tm, tm), const2),
        ],
        out_specs=(
            pl.BlockSpec((1, tm, d), tile3),
            pl.BlockSpec((tm * rpt, LANES), lambda b, s: (b * n_s + s, 0)),
            pl.BlockSpec((1, tm, LANES), tile3),
            pl.BlockSpec((SUBLANES, LANES), const2),
        ),
        scratch_shapes=[pltpu.VMEM((SUBLANES, LANES), F32)],
        compiler_params=pltpu.CompilerParams(
            dimension_semantics=("arbitrary", "arbitrary"), vmem_limit_bytes=VMEM_LIMIT),
        name="outproj",
    )(x, attn, conv, mod, ga, wout, g2, wr, br, tri)


def _dest_kernel(route_ref, cnt_ref, dest_out):
    cnt = cnt_ref[...]
    lane8 = lax.broadcasted_iota(jnp.int32, (SUBLANES, LANES), 1)
    inc = cnt
    shift = 1
    while shift < N_EXPERTS:
        inc = inc + jnp.where(lane8 >= shift, pltpu.roll(inc, shift, axis=1), 0.0)
        shift *= 2
    starts = (inc - cnt)[0:1, :]
    r = route_ref[...]
    tm = r.shape[0]
    lane = lax.broadcasted_iota(jnp.int32, (tm, LANES), 1).astype(F32)
    d1 = r[:, 2:3] + jnp.sum(jnp.where(lane == r[:, 0:1], starts, 0.0), axis=-1, keepdims=True)
    d2 = r[:, 3:4] + jnp.sum(jnp.where(lane == r[:, 1:2], starts, 0.0), axis=-1, keepdims=True)
    packed = jnp.where(lane == 0.0, d1, jnp.where(lane == 1.0, d2, 0.0))
    dest_out[...] = packed.T[0:SUBLANES, :].astype(jnp.int32)


def _dest(route, cnt):
    t = route.shape[0]
    tm = TM_DEST
    return pl.pallas_call(
        _dest_kernel,
        out_shape=jax.ShapeDtypeStruct((SUBLANES, t), jnp.int32),
        grid=(t // tm,),
        in_specs=[
            pl.BlockSpec((tm, LANES), lambda i: (i, 0)),
            pl.BlockSpec((SUBLANES, LANES), lambda i: (0, 0)),
        ],
        out_specs=pl.BlockSpec((SUBLANES, tm), lambda i: (0, i)),
        compiler_params=pltpu.CompilerParams(
            dimension_semantics=("arbitrary",), vmem_limit_bytes=VMEM_LIMIT),
        name="dest",
    )(route, cnt)


def _token_copy(src, src_tok, dst, dst_tok, sem, rpt):
    return pltpu.make_async_copy(src.at[pl.ds(src_tok * rpt, rpt)], dst.at[pl.ds(dst_tok * rpt, rpt)],
                                 sem)


def _dispatch_kernel(dest_ref, h_ref, xs_hbm, sem, *, tm, rpt):
    def issue(t, carry):
        for k in range(TOP_K):
            _token_copy(h_ref, t, xs_hbm, dest_ref[k, t], sem, rpt).start()
        return carry

    lax.fori_loop(0, tm, issue, 0, unroll=8)

    def drain(t, carry):
        for k in range(TOP_K):
            _token_copy(h_ref, 0, xs_hbm, 0, sem, rpt).wait()
        return carry

    lax.fori_loop(0, tm, drain, 0, unroll=8)


def _dispatch(dest, h, rpt):
    t = h.shape[0] // rpt
    tm = TM_DISP
    return pl.pallas_call(
        functools.partial(_dispatch_kernel, tm=tm, rpt=rpt),
        out_shape=jax.ShapeDtypeStruct((t * TOP_K * rpt, LANES), h.dtype),
        grid=(t // tm,),
        in_specs=[
            pl.BlockSpec((SUBLANES, tm), lambda i: (0, i), memory_space=pltpu.SMEM),
            pl.BlockSpec((tm * rpt, LANES), lambda i: (i, 0)),
        ],
        out_specs=pl.BlockSpec(memory_space=pl.ANY),
        scratch_shapes=[pltpu.SemaphoreType.DMA],
        compiler_params=pltpu.CompilerParams(
            dimension_semantics=("arbitrary",), vmem_limit_bytes=VMEM_LIMIT),
        name="dispatch",
    )(dest, h)


def _experts_kernel(blk_ref, exp_ref, lo_ref, hi_ref, x_ref, wgu_ref, wd_ref, o_ref, acc_ref,
                    *, n_items, rpt):
    i = pl.program_id(0)
    lo = lo_ref[i]
    hi = hi_ref[i]
    blk = blk_ref[i]
    first = (i == 0) | (blk != blk_ref[jnp.maximum(i - 1, 0)])
    last = (i == n_items - 1) | (blk != blk_ref[jnp.minimum(i + 1, n_items - 1)])

    @pl.when(first)
    def _():
        acc_ref[...] = jnp.zeros(acc_ref.shape, F32)

    @pl.when(hi > lo)
    def _():
        x = _load_token_rows(x_ref, BLK, rpt).astype(BF16)
        gu = _dot(x, wgu_ref[0])
        de = gu.shape[1] // 2
        g = gu[:, :de]
        a = (g * jax.nn.sigmoid(g) * gu[:, de:]).astype(BF16)
        y = _dot(a, wd_ref[0])
        row = lax.broadcasted_iota(jnp.int32, (BLK, 1), 0)
        keep = (row >= lo) & (row < hi)
        acc_ref[...] += jnp.where(keep, y, 0.0)

    @pl.when(last)
    def _():
        _store_token_rows(o_ref, acc_ref[...])


def _experts(item_blk, item_exp, item_lo, item_hi, xs, wgu, wd, rpt):
    a = xs.shape[0] // rpt
    d = rpt * LANES
    de = wd.shape[1]
    n_items = item_blk.shape[0]
    grid_spec = pltpu.PrefetchScalarGridSpec(
        num_scalar_prefetch=4,
        grid=(n_items,),
        in_specs=[
            pl.BlockSpec((BLK * rpt, LANES), lambda i, blk, ex, lo, hi: (blk[i], 0)),
            pl.BlockSpec((1, d, 2 * de), lambda i, blk, ex, lo, hi: (ex[i], 0, 0)),
            pl.BlockSpec((1, de, d), lambda i, blk, ex, lo, hi: (ex[i], 0, 0)),
        ],
        out_specs=pl.BlockSpec((BLK * rpt, LANES), lambda i, blk, ex, lo, hi: (blk[i], 0)),
        scratch_shapes=[pltpu.VMEM((BLK, d), F32)],
    )
    return pl.pallas_call(
        functools.partial(_experts_kernel, n_items=n_items, rpt=rpt),
        out_shape=jax.ShapeDtypeStruct((a * rpt, LANES), F32),
        grid_spec=grid_spec,
        compiler_params=pltpu.CompilerParams(
            dimension_semantics=("arbitrary",), vmem_limit_bytes=VMEM_LIMIT),
        name="experts",
    )(item_blk, item_exp, item_lo, item_hi, xs, wgu, wd)


def _work_items(counts, n_rows):
    n_blocks = n_rows // BLK
    n_items = n_blocks + N_EXPERTS - 1
    counts = counts.astype(jnp.int32)
    ends = jnp.cumsum(counts)
    starts = ends - counts
    first_blk = starts // BLK
    last_blk = jnp.maximum(ends - 1, 0) // BLK
    n_e = jnp.where(counts > 0, last_blk - first_blk + 1, 0)
    item_end = jnp.cumsum(n_e)
    item_start = item_end - n_e
    idx = jnp.arange(n_items, dtype=jnp.int32)
    total = item_end[-1]
    ex = jnp.minimum(jnp.sum((idx[:, None] >= item_end[None, :]).astype(jnp.int32), axis=1),
                     N_EXPERTS - 1)
    onehot = (ex[:, None] == jnp.arange(N_EXPERTS, dtype=jnp.int32)[None, :]).astype(jnp.int32)
    pick = lambda table: jnp.sum(onehot * table[None, :], axis=1)
    blk = pick(first_blk) + (idx - pick(item_start))
    lo = jnp.maximum(pick(starts), blk * BLK) - blk * BLK
    hi = jnp.minimum(pick(ends), (blk + 1) * BLK) - blk * BLK
    valid = idx < total
    last_ex = jnp.max(jnp.where(valid, ex, 0))
    blk = jnp.where(valid, blk, n_blocks - 1).astype(jnp.int32)
    ex = jnp.where(valid, ex, last_ex).astype(jnp.int32)
    lo = jnp.where(valid, lo, 0).astype(jnp.int32)
    hi = jnp.where(valid, hi, 0).astype(jnp.int32)
    return blk, ex, lo, hi


def _combine_kernel(dest_ref, dest_next_ref, ys_hbm, x1_ref, route_ref, mod_ref, gf_ref, o_ref,
                    ybuf, sems, *, tm, n_steps, final_norm, rpt):
    i = pl.program_id(0)
    slot = i % 2

    def gather(d_ref, slot_idx):
        def body(t, carry):
            for k in range(TOP_K):
                _token_copy(ys_hbm, d_ref[k, t], ybuf.at[slot_idx, k], t, sems.at[slot_idx],
                            rpt).start()
            return carry
        lax.fori_loop(0, tm, body, 0, unroll=8)

    @pl.when(i == 0)
    def _():
        gather(dest_ref, 0)

    @pl.when(i + 1 < n_steps)
    def _():
        gather(dest_next_ref, 1 - slot)

    def drain(t, carry):
        for k in range(TOP_K):
            _token_copy(ys_hbm, 0, ybuf.at[slot, k], 0, sems.at[slot], rpt).wait()
        return carry

    lax.fori_loop(0, tm, drain, 0, unroll=8)

    r = route_ref[...]
    y = (r[:, 4:5] * _load_token_rows(ybuf.at[slot, 0], tm, rpt)
         + r[:, 5:6] * _load_token_rows(ybuf.at[slot, 1], tm, rpt))
    mod = mod_ref[0]
    x2 = x1_ref[...] + mod[5:6] * y
    o_ref[...] = _rms(x2, gf_ref[...]) if final_norm else x2


def _combine(dest, ys, x1, route, mod, gf, seq, final_norm):
    t, d = x1.shape
    tm = TM_COMB
    n_steps = t // tm
    per_b = seq // tm
    rpt = d // LANES
    return pl.pallas_call(
        functools.partial(_combine_kernel, tm=tm, n_steps=n_steps, final_norm=final_norm, rpt=rpt),
        out_shape=jax.ShapeDtypeStruct((t, d), F32),
        grid=(n_steps,),
        in_specs=[
            pl.BlockSpec((SUBLANES, tm), lambda i: (0, i), memory_space=pltpu.SMEM),
            pl.BlockSpec((SUBLANES, tm), lambda i: (0, jnp.minimum(i + 1, n_steps - 1)),
                         memory_space=pltpu.SMEM),
            pl.BlockSpec(memory_space=pl.ANY),
            pl.BlockSpec((tm, d), lambda i: (i, 0)),
            pl.BlockSpec((tm, LANES), lambda i: (i, 0)),
            pl.BlockSpec((1, N_MOD, d), lambda i: (i // per_b, 0, 0)),
            pl.BlockSpec((1, d), lambda i: (0, 0)),
        ],
        out_specs=pl.BlockSpec((tm, d), lambda i: (i, 0)),
        scratch_shapes=[pltpu.VMEM((2, TOP_K, tm * rpt, LANES), F32),
                        pltpu.SemaphoreType.DMA((2,))],
        compiler_params=pltpu.CompilerParams(
            dimension_semantics=("arbitrary",), vmem_limit_bytes=VMEM_LIMIT),
        name="combine",
    )(dest, dest, ys, x1, route, mod, gf)


def _prep_in_weights(w_in, q_rank, kv_rank, conv_dim):
    o1 = q_rank
    o2 = o1 + kv_rank
    o3 = o2 + D_ROPE
    d = w_in.shape[0]
    w_kr = w_in[:, o2:o3]
    half = D_ROPE // 2
    w_kr_rot = jnp.concatenate([-w_kr[:, half:], w_kr[:, :half]], axis=1)
    zl = jnp.zeros((d, D_NOPE), w_in.dtype)
    zr = jnp.zeros((d, LANES - D_NOPE - D_ROPE), w_in.dtype)
    return jnp.concatenate(
        [w_in[:, :o2], zl, w_kr, zr, zl, w_kr_rot, zr, w_in[:, o3:]], axis=1).astype(BF16)


def _prep_q_weights(w_uq):
    r = w_uq.shape[0]
    w = w_uq.reshape(r, HEADS, D_NOPE + D_ROPE)
    nope, rope = w[..., :D_NOPE], w[..., D_NOPE:]
    half = D_ROPE // 2
    rot = jnp.concatenate([-rope[..., half:], rope[..., :half]], axis=-1)
    zr = jnp.zeros((r, HEADS, HEAD_PAD - D_NOPE - D_ROPE), w_uq.dtype)
    w1 = jnp.concatenate([nope, rope, zr], axis=-1).reshape(r, HEADS * HEAD_PAD)
    w2 = jnp.concatenate([jnp.zeros_like(nope), rot, zr], axis=-1).reshape(r, HEADS * HEAD_PAD)
    return w1.astype(BF16), w2.astype(BF16)


def _prep_k_weights(w_uk):
    r = w_uk.shape[0]
    w = w_uk.reshape(r, HEADS, D_NOPE)
    zr = jnp.zeros((r, HEADS, HEAD_PAD - D_NOPE), w_uk.dtype)
    return jnp.concatenate([w, zr], axis=-1).reshape(r, HEADS * HEAD_PAD).astype(BF16)


def _rope_freqs():
    inv_freq = 1.0 / (ROPE_THETA ** (jnp.arange(0, D_ROPE, 2, dtype=F32) / D_ROPE))
    return inv_freq.reshape(D_ROPE // 2, 1)


def kernel(x, c, positions, w_ada, b_ada, norm1_g, w_in, q_norm_g, kv_norm_g, w_uq, w_uk, w_uv,
           conv_w, conv_b, attn_out_g, conv_out_g, w_out, norm2_g, w_router_group, b_router_group,
           w_router_expert, b_router_expert, w_exp_gate, w_exp_up, w_exp_down, final_g):
    bsz, seq, d = x.shape
    depth = w_ada.shape[0]
    q_rank = q_norm_g.shape[-1]
    kv_rank = kv_norm_g.shape[-1]
    conv_dim = conv_w.shape[-1]
    t = bsz * seq
    assert seq % max(TM_PROJ, TQ, TM_OUT, TM_COMB) == 0 and TQ % TK == 0 and TM_PROJ % TK == 0
    assert CHUNK & (CHUNK - 1) == 0 and TQ % CHUNK == 0
    assert t % max(TM_DEST, TM_DISP, TM_COMB) == 0 and (t * TOP_K) % BLK == 0
    assert w_uq.shape[-1] == HEADS * (D_NOPE + D_ROPE) and w_uv.shape[-1] == HEADS * D_V

    pos = positions.reshape(bsz, 1, seq)
    invf = _rope_freqs()
    row = lambda v: v.reshape(1, -1)
    for l in range(depth):
        mod = _adaln(c, w_ada[l], b_ada[l]).reshape(bsz, N_MOD, d)
        win = _prep_in_weights(w_in[l], q_rank, kv_rank, conv_dim)
        wq1, wq2 = _prep_q_weights(w_uq[l])
        wk = _prep_k_weights(w_uk[l])
        wvt = w_uv[l].T.astype(BF16)
        q, k, vt, conv = _inproj(x, mod, pos, row(norm1_g[l]), win, row(q_norm_g[l]),
                                 row(kv_norm_g[l]), wq1, wq2, wk, wvt, invf, conv_w[l],
                                 row(conv_b[l]), row(conv_out_g[l]))
        attn = _attention(q, k, vt)

        wr = jnp.concatenate(
            [w_router_expert[l], w_router_group[l],
             jnp.zeros((d, LANES - N_EXPERTS - N_GROUPS), F32)], axis=1).astype(BF16)
        br = jnp.concatenate(
            [b_router_expert[l], b_router_group[l],
             jnp.zeros((LANES - N_EXPERTS - N_GROUPS,), F32)]).reshape(1, LANES)
        x1, h2, route, cnt = _outproj(x, attn, conv, mod, row(attn_out_g[l]),
                                      w_out[l].astype(BF16), row(norm2_g[l]), wr, br)
        route = route.reshape(t, LANES)
        dest = _dest(route, cnt)
        rpt = d // LANES
        xs = _dispatch(dest, h2, rpt)
        items = _work_items(cnt[0, :N_EXPERTS], t * TOP_K)
        wgu = jnp.concatenate([w_exp_gate[l], w_exp_up[l]], axis=-1).astype(BF16)
        ys = _experts(*items, xs, wgu, w_exp_down[l].astype(BF16), rpt)
        out = _combine(dest, ys, x1.reshape(t, d), route, mod, row(final_g), seq, l == depth - 1)
        x = out.reshape(bsz, seq, d)
    return x
```

```python
import functools
import math

import jax
import jax.numpy as jnp
from jax import lax
from jax.experimental import pallas as pl
from jax.experimental.pallas import tpu as pltpu

F32 = jnp.float32
BF16 = jnp.bfloat16

CHUNK = 64
HEADS = 8
D_NOPE = 64
D_ROPE = 32
D_V = 64
ROPE_THETA = 10000.0
CONV_WIDTH = 3
N_GROUPS = 4
EXPERTS_PER_GROUP = 8
N_EXPERTS = N_GROUPS * EXPERTS_PER_GROUP
TOP_K = 2
N_MOD = 6
RMS_EPS = 1e-6
NEG_INF = -1e30
LOG2_E = math.log2(math.e)

LANES = 128
SUBLANES = 8
HEAD_PAD = LANES
DEN_ROWS = 16
VMEM_LIMIT = 56 * 1024 * 1024

TM_PROJ = 512
TQ = 512
TK = 256
TM_OUT = 1024
TM_DEST = 2048
TM_DISP = 512
BLK = 256
SUB = 256
TM_COMB = 256
TN_ADA = 512


def _rms(x, g):
    return x * lax.rsqrt(jnp.mean(x * x, axis=-1, keepdims=True) + RMS_EPS) * g


def _dot(a, b):
    return jnp.dot(a, b, preferred_element_type=F32)


def _dot_nt(a, b):
    return lax.dot_general(a, b, (((1,), (1,)), ((), ())), preferred_element_type=F32)


def _store_token_rows(ref, x):
    n, d = x.shape
    r = d // LANES
    for j in range(r):
        ref[pl.ds(j, n, stride=r), :] = x[:, j * LANES:(j + 1) * LANES]


def _load_token_rows(ref, n, r):
    return jnp.concatenate([ref[pl.ds(j, n, stride=r), :] for j in range(r)], axis=1)


def _split_bf16(x):
    hi = x.astype(BF16)
    lo = (x - hi.astype(F32)).astype(BF16)
    return hi, lo


def _adaln_kernel(c_ref, w_ref, b_ref, o_ref):
    c = c_ref[...]
    ca = c * jax.nn.sigmoid(c)
    c_hi, c_lo = _split_bf16(ca)
    w_hi, w_lo = _split_bf16(w_ref[...])
    acc = _dot(c_hi, w_hi) + _dot(c_hi, w_lo) + _dot(c_lo, w_hi)
    o_ref[...] = acc + b_ref[...]


def _adaln(c, w, b):
    bsz, d = c.shape
    n = w.shape[1]
    return pl.pallas_call(
        _adaln_kernel,
        out_shape=jax.ShapeDtypeStruct((bsz, n), F32),
        grid=(n // TN_ADA,),
        in_specs=[
            pl.BlockSpec((bsz, d), lambda j: (0, 0)),
            pl.BlockSpec((d, TN_ADA), lambda j: (0, j)),
            pl.BlockSpec((1, TN_ADA), lambda j: (0, j)),
        ],
        out_specs=pl.BlockSpec((bsz, TN_ADA), lambda j: (0, j)),
        compiler_params=pltpu.CompilerParams(
            dimension_semantics=("arbitrary",), vmem_limit_bytes=VMEM_LIMIT),
        name="adaln",
    )(c, w, b.reshape(1, n))


def _inproj_kernel(x_ref, mod_ref, pos_ref, g1_ref, win_ref, gq_ref, gkv_ref, wq1_ref, wq2_ref,
                   wk_ref, wvt_ref, invf_ref, cw_ref, cb_ref, gc_ref,
                   q_out, k_out, vt_out, conv_out, zbuf, *, tm, q_rank, kv_rank, conv_dim):
    si = pl.program_id(1)
    x = x_ref[0]
    mod = mod_ref[0]
    h = (_rms(x, g1_ref[...]) * (1.0 + mod[1:2]) + mod[0:1]).astype(BF16)
    proj = _dot(h, win_ref[...])

    o_kv = q_rank
    o_kr = o_kv + kv_rank
    o_u = o_kr + 2 * LANES
    q_c = proj[:, 0:q_rank]
    kv_c = proj[:, o_kv:o_kv + kv_rank]
    kr = proj[:, o_kr:o_kr + LANES]
    kr_rot = proj[:, o_kr + LANES:o_kr + 2 * LANES]
    u = proj[:, o_u:o_u + conv_dim]
    b_gate = proj[:, o_u + conv_dim:o_u + 2 * conv_dim]
    c_gate = proj[:, o_u + 2 * conv_dim:o_u + 3 * conv_dim]

    ang_t = invf_ref[...] * pos_ref[0].astype(F32)

    def head_lanes(tab_t):
        zl = jnp.zeros((D_NOPE, tm), F32)
        zr = jnp.zeros((LANES - D_NOPE - D_ROPE, tm), F32)
        return jnp.concatenate([zl, tab_t, tab_t, zr], axis=0).T

    cos_t = head_lanes(jnp.cos(ang_t))
    sin_t = head_lanes(jnp.sin(ang_t))
    lane = lax.broadcasted_iota(jnp.int32, (1, LANES), 1)
    q_scale = LOG2_E / math.sqrt(D_NOPE + D_ROPE)
    q_cos = jnp.where(lane < D_NOPE, q_scale, cos_t * q_scale)
    q_sin = sin_t * q_scale

    qn = _rms(q_c, gq_ref[...]).astype(BF16)
    qa = _dot(qn, wq1_ref[...])
    qb = _dot(qn, wq2_ref[...])
    kvn = _rms(kv_c, gkv_ref[...]).astype(BF16)
    kn = _dot(kvn, wk_ref[...])
    k_rope = kr * cos_t + kr_rot * sin_t
    for hd in range(HEADS):
        sl = slice(hd * HEAD_PAD, (hd + 1) * HEAD_PAD)
        q_out[0, hd] = (qa[:, sl] * q_cos + qb[:, sl] * q_sin).astype(BF16)
        k_out[0, hd] = (kn[:, sl] + k_rope).astype(BF16)

    vt = _dot_nt(wvt_ref[...], kvn)
    for j in range(tm // TK):
        vt_out[0, j] = vt[:, j * TK:(j + 1) * TK].astype(BF16)

    z = c_gate * u

    @pl.when(si == 0)
    def _():
        zbuf[0:SUBLANES, :] = jnp.zeros((SUBLANES, conv_dim), F32)

    zbuf[SUBLANES:SUBLANES + tm, :] = z
    z1 = zbuf[SUBLANES - 1:SUBLANES - 1 + tm, :]
    z2 = zbuf[SUBLANES - 2:SUBLANES - 2 + tm, :]
    cw = cw_ref[...]
    zc = cw[0:1] * z2 + cw[1:2] * z1 + cw[2:3] * z + cb_ref[...]
    conv = b_gate * zc
    conv_out[0] = _rms(conv, gc_ref[...]).astype(BF16)
    zbuf[0:SUBLANES, :] = zbuf[tm:tm + SUBLANES, :]


def _inproj(x, mod, pos, g1, win, gq, gkv, wq1, wq2, wk, wvt, invf, cw, cb, gc):
    bsz, seq, d = x.shape
    tm = TM_PROJ
    q_rank, kv_rank = wq1.shape[0], wk.shape[0]
    conv_dim = cw.shape[1]
    hv = wvt.shape[0]
    n_s = seq // tm
    const2 = lambda b, s: (0, 0)
    kern = functools.partial(_inproj_kernel, tm=tm, q_rank=q_rank, kv_rank=kv_rank,
                             conv_dim=conv_dim)
    return pl.pallas_call(
        kern,
        out_shape=(
            jax.ShapeDtypeStruct((bsz, HEADS, seq, HEAD_PAD), BF16),
            jax.ShapeDtypeStruct((bsz, HEADS, seq, HEAD_PAD), BF16),
            jax.ShapeDtypeStruct((bsz, seq // TK, hv, TK), BF16),
            jax.ShapeDtypeStruct((bsz, seq, conv_dim), BF16),
        ),
        grid=(bsz, n_s),
        in_specs=[
            pl.BlockSpec((1, tm, d), lambda b, s: (b, s, 0)),
            pl.BlockSpec((1, N_MOD, d), lambda b, s: (b, 0, 0)),
            pl.BlockSpec((1, 1, tm), lambda b, s: (b, 0, s)),
            pl.BlockSpec((1, d), const2),
            pl.BlockSpec(win.shape, const2),
            pl.BlockSpec((1, q_rank), const2),
            pl.BlockSpec((1, kv_rank), const2),
            pl.BlockSpec(wq1.shape, const2),
            pl.BlockSpec(wq2.shape, const2),
            pl.BlockSpec(wk.shape, const2),
            pl.BlockSpec(wvt.shape, const2),
            pl.BlockSpec(invf.shape, const2),
            pl.BlockSpec(cw.shape, const2),
            pl.BlockSpec((1, conv_dim), const2),
            pl.BlockSpec((1, conv_dim), const2),
        ],
        out_specs=(
            pl.BlockSpec((1, HEADS, tm, HEAD_PAD), lambda b, s: (b, 0, s, 0)),
            pl.BlockSpec((1, HEADS, tm, HEAD_PAD), lambda b, s: (b, 0, s, 0)),
            pl.BlockSpec((1, tm // TK, hv, TK), lambda b, s: (b, s, 0, 0)),
            pl.BlockSpec((1, tm, conv_dim), lambda b, s: (b, s, 0)),
        ),
        scratch_shapes=[pltpu.VMEM((tm + 2 * SUBLANES, conv_dim), F32)],
        compiler_params=pltpu.CompilerParams(
            dimension_semantics=("arbitrary", "arbitrary"), vmem_limit_bytes=VMEM_LIMIT),
        name="inproj",
    )(x, mod, pos, g1, win, gq, gkv, wq1, wq2, wk, wvt, invf, cw, cb, gc)


def _attn_kernel(q_ref, k_ref, vt_ref, o_ref, *, n_q):
    row = lax.broadcasted_iota(jnp.int32, (TQ, TQ), 0)
    col = lax.broadcasted_iota(jnp.int32, (TQ, TQ), 1)
    shift = CHUNK.bit_length() - 1
    diag_ok = (row >> shift) <= (col >> shift)
    vt_per_step = TQ // TK

    ones_rows = jnp.ones((DEN_ROWS, TQ), BF16)

    def scores(hh, qi, j):
        return _dot_nt(k_ref[0, hh, j * TQ:(j + 1) * TQ, :], q_ref[0, hh, qi * TQ:(qi + 1) * TQ, :])

    def softmax_pv(hh, j, s_t, state, masked):
        m, acc = state
        if masked:
            s_t = jnp.where(diag_ok, s_t, NEG_INF)
        m_new = jnp.maximum(m, jnp.max(s_t, axis=0, keepdims=True))
        p = jnp.exp2(s_t - m_new).astype(BF16)
        alpha = jnp.exp2(m - m_new)
        v_t = jnp.concatenate(
            [vt_ref[0, j * vt_per_step + i, hh * D_V:(hh + 1) * D_V, :]
             for i in range(vt_per_step)], axis=1)
        v_aug = jnp.concatenate([v_t, ones_rows], axis=0)
        acc_new = alpha * acc + _dot(v_aug, p)
        return m_new, acc_new

    steps = [(qi, j) for qi in range(n_q) for j in range(qi + 1)]
    init = (jnp.full((1, TQ), NEG_INF, F32), jnp.zeros((D_V + DEN_ROWS, TQ), F32))
    s_cur = tuple(scores(hh, *steps[0]) for hh in range(2))
    states = None
    for idx, (qi, j) in enumerate(steps):
        s_next = None
        if idx + 1 < len(steps):
            s_next = tuple(scores(hh, *steps[idx + 1]) for hh in range(2))
        if j == 0:
            states = (init, init)
        states = tuple(softmax_pv(hh, j, s_cur[hh], states[hh], j == qi) for hh in range(2))
        if j == qi:
            o_t = jnp.concatenate([acc[:D_V] / acc[D_V:D_V + 1] for (_, acc) in states],
                                  axis=0)
            o_ref[0, qi * TQ:(qi + 1) * TQ, :] = o_t.T.astype(BF16)
        s_cur = s_next


def _attention(q, k, vt):
    bsz, _, seq, _ = q.shape
    n_pair = HEADS // 2
    return pl.pallas_call(
        functools.partial(_attn_kernel, n_q=seq // TQ),
        out_shape=jax.ShapeDtypeStruct((bsz, seq, HEADS * D_V), BF16),
        grid=(bsz, n_pair),
        in_specs=[
            pl.BlockSpec((1, 2, seq, HEAD_PAD), lambda b, p: (b, p, 0, 0)),
            pl.BlockSpec((1, 2, seq, HEAD_PAD), lambda b, p: (b, p, 0, 0)),
            pl.BlockSpec((1, seq // TK, 2 * D_V, TK), lambda b, p: (b, 0, p, 0)),
        ],
        out_specs=pl.BlockSpec((1, seq, 2 * D_V), lambda b, p: (b, 0, p)),
        compiler_params=pltpu.CompilerParams(
            dimension_semantics=("arbitrary", "arbitrary"), vmem_limit_bytes=VMEM_LIMIT),
        name="attn",
    )(q, k, vt)


def _outproj_kernel(x_ref, attn_ref, conv_ref, mod_ref, ga_ref, wout_ref, g2_ref, wr_ref, br_ref,
                    tri_ref, x1_out, h2_out, route_out, rt_out, cnt_out, base_ref):
    first = (pl.program_id(0) == 0) & (pl.program_id(1) == 0)

    @pl.when(first)
    def _():
        base_ref[...] = jnp.zeros(base_ref.shape, F32)

    mod = mod_ref[0]
    an = _rms(attn_ref[0].astype(F32), ga_ref[...]).astype(BF16)
    mixed = jnp.concatenate([an, conv_ref[0]], axis=-1)
    y = _dot(mixed, wout_ref[...])
    x1 = x_ref[0] + mod[2:3] * y
    x1_out[0] = x1
    h2 = _rms(x1, g2_ref[...]) * (1.0 + mod[4:5]) + mod[3:4]
    _store_token_rows(h2_out, h2)
    logits = _dot(h2.astype(BF16), wr_ref[...]) + br_ref[...]

    tm = logits.shape[0]
    lane = lax.broadcasted_iota(jnp.int32, (tm, LANES), 1).astype(F32)
    big = float(LANES)
    is_grp = (lane >= N_EXPERTS) & (lane < N_EXPERTS + N_GROUPS)
    gl = jnp.where(is_grp, logits, -jnp.inf)
    gmax = jnp.max(gl, axis=-1, keepdims=True)
    gidx = jnp.min(jnp.where(gl == gmax, lane, big), axis=-1, keepdims=True) - N_EXPERTS
    p_group = 1.0 / jnp.sum(jnp.where(is_grp, jnp.exp(logits - gmax), 0.0), axis=-1, keepdims=True)
    lo = gidx * EXPERTS_PER_GROUP
    in_grp = (lane >= lo) & (lane < lo + EXPERTS_PER_GROUP)
    el = jnp.where(in_grp, logits, -jnp.inf)
    v1 = jnp.max(el, axis=-1, keepdims=True)
    i1 = jnp.min(jnp.where(el == v1, lane, big), axis=-1, keepdims=True)
    el2 = jnp.where(lane == i1, -jnp.inf, el)
    v2 = jnp.max(el2, axis=-1, keepdims=True)
    i2 = jnp.min(jnp.where(el2 == v2, lane, big), axis=-1, keepdims=True)
    t = jnp.exp(v2 - v1)
    w1 = p_group / (1.0 + t)
    w2 = p_group * t / (1.0 + t)

    oh1 = lane == i1
    oh2 = lane == i2
    cnt = jnp.where(oh1 | oh2, 1.0, 0.0)
    prefix = _dot(tri_ref[...], cnt.astype(BF16)) + base_ref[0:1, :]
    r1 = jnp.sum(jnp.where(oh1, prefix, 0.0), axis=-1, keepdims=True)
    r2 = jnp.sum(jnp.where(oh2, prefix, 0.0), axis=-1, keepdims=True)
    new_base = base_ref[...] + jnp.sum(cnt, axis=0, keepdims=True)
    base_ref[...] = new_base
    cnt_out[...] = new_base

    route = jnp.where(lane == 0.0, i1, jnp.where(lane == 1.0, i2, jnp.where(
        lane == 2.0, r1, jnp.where(lane == 3.0, r2, jnp.where(
            lane == 4.0, w1, jnp.where(lane == 5.0, w2, 0.0))))))
    route_out[0] = route
    rt_out[...] = route.T[0:SUBLANES, :].astype(jnp.int32)


def _outproj(x, attn, conv, mod, ga, wout, g2, wr, br):
    bsz, seq, d = x.shape
    tm = TM_OUT
    aw = attn.shape[-1]
    cwd = conv.shape[-1]
    n_s = seq // tm
    rpt = d // LANES
    tri = (lax.broadcasted_iota(jnp.int32, (tm, tm), 1)
           < lax.broadcasted_iota(jnp.int32, (tm, tm), 0)).astype(BF16)
    const2 = lambda b, s: (0, 0)
    tile3 = lambda b, s: (b, s, 0)
    return pl.pallas_call(
        _outproj_kernel,
        out_shape=(
            jax.ShapeDtypeStruct((bsz, seq, d), F32),
            jax.ShapeDtypeStruct((bsz * seq * rpt, LANES), F32),
            jax.ShapeDtypeStruct((bsz, seq, LANES), F32),
            jax.ShapeDtypeStruct((SUBLANES, bsz * seq), jnp.int32),
            jax.ShapeDtypeStruct((SUBLANES, LANES), F32),
        ),
        grid=(bsz, n_s),
        in_specs=[
            pl.BlockSpec((1, tm, d), tile3),
            pl.BlockSpec((1, tm, aw), tile3),
            pl.BlockSpec((1, tm, cwd), tile3),
            pl.BlockSpec((1, N_MOD, d), lambda b, s: (b, 0, 0)),
            pl.BlockSpec((1, aw), const2),
            pl.BlockSpec(wout.shape, const2),
            pl.BlockSpec((1, d), const2),
            pl.BlockSpec(wr.shape, const2),
            pl.BlockSpec((1, LANES), const2),
            pl.BlockSpec((tm, tm), const2),
        ],
        out_specs=(
            pl.BlockSpec((1, tm, d), tile3),
            pl.BlockSpec((tm * rpt, LANES), lambda b, s: (b * n_s + s, 0)),
            pl.BlockSpec((1, tm, LANES), tile3),
            pl.BlockSpec((SUBLANES, tm), lambda b, s: (0, b * n_s + s)),
            pl.BlockSpec((SUBLANES, LANES), const2),
        ),
        scratch_shapes=[pltpu.VMEM((SUBLANES, LANES), F32)],
        compiler_params=pltpu.CompilerParams(
            dimension_semantics=("arbitrary", "arbitrary"), vmem_limit_bytes=VMEM_LIMIT),
        name="outproj",
    )(x, attn, conv, mod, ga, wout, g2, wr, br, tri)


def _dest_kernel(starts_ref, rt_ref, dest_out):
    rt = rt_ref[...]
    ex = rt[0:TOP_K, :]
    start = jnp.zeros(ex.shape, jnp.int32)
    for e in range(N_EXPERTS):
        start = jnp.where(ex == e, starts_ref[e], start)
    dest = start + rt[TOP_K:2 * TOP_K, :]
    dest_out[...] = jnp.concatenate(
        [dest, jnp.zeros((SUBLANES - TOP_K, dest.shape[1]), jnp.int32)], axis=0)


def _dest(starts, rt):
    t = rt.shape[1]
    tm = TM_DEST
    grid_spec = pltpu.PrefetchScalarGridSpec(
        num_scalar_prefetch=1,
        grid=(t // tm,),
        in_specs=[pl.BlockSpec((SUBLANES, tm), lambda i, st: (0, i))],
        out_specs=pl.BlockSpec((SUBLANES, tm), lambda i, st: (0, i)),
    )
    return pl.pallas_call(
        _dest_kernel,
        out_shape=jax.ShapeDtypeStruct((SUBLANES, t), jnp.int32),
        grid_spec=grid_spec,
        compiler_params=pltpu.CompilerParams(
            dimension_semantics=("arbitrary",), vmem_limit_bytes=VMEM_LIMIT),
        name="dest",
    )(starts, rt)


def _token_copy(src, src_tok, dst, dst_tok, sem, rpt):
    return pltpu.make_async_copy(src.at[pl.ds(src_tok * rpt, rpt)], dst.at[pl.ds(dst_tok * rpt, rpt)],
                                 sem)


def _dispatch_kernel(dest_ref, h_ref, xs_hbm, sem, *, tm, rpt):
    def issue(t, carry):
        for k in range(TOP_K):
            _token_copy(h_ref, t, xs_hbm, dest_ref[k, t], sem, rpt).start(priority=k)
        return carry

    lax.fori_loop(0, tm, issue, 0, unroll=8)

    def drain(t, carry):
        for k in range(TOP_K):
            _token_copy(h_ref, 0, xs_hbm, 0, sem, rpt).wait()
        return carry

    lax.fori_loop(0, tm, drain, 0, unroll=8)


def _dispatch(dest, h, rpt):
    t = h.shape[0] // rpt
    tm = TM_DISP
    return pl.pallas_call(
        functools.partial(_dispatch_kernel, tm=tm, rpt=rpt),
        out_shape=jax.ShapeDtypeStruct((t * TOP_K * rpt, LANES), h.dtype),
        grid=(t // tm,),
        in_specs=[
            pl.BlockSpec((SUBLANES, tm), lambda i: (0, i), memory_space=pltpu.SMEM),
            pl.BlockSpec((tm * rpt, LANES), lambda i: (i, 0)),
        ],
        out_specs=pl.BlockSpec(memory_space=pl.ANY),
        scratch_shapes=[pltpu.SemaphoreType.DMA],
        compiler_params=pltpu.CompilerParams(
            dimension_semantics=("arbitrary",), vmem_limit_bytes=VMEM_LIMIT),
        name="dispatch",
    )(dest, h)


def _experts_kernel(blk_ref, exp_ref, lo_ref, hi_ref, x_ref, wgu_ref, wd_ref, o_ref, acc_ref,
                    *, n_items, rpt):
    i = pl.program_id(0)
    lo = lo_ref[i]
    hi = hi_ref[i]
    blk = blk_ref[i]
    first = (i == 0) | (blk != blk_ref[jnp.maximum(i - 1, 0)])

    @pl.when(i == 0)
    def _():
        acc_ref[...] = jnp.zeros(acc_ref.shape, F32)

    @pl.when(hi > lo)
    def _():
        for sb in range(BLK // SUB):
            rows = pl.ds(sb * SUB, SUB)
            tok_rows = pl.ds(sb * SUB * rpt, SUB * rpt)
            x = _load_token_rows(x_ref.at[tok_rows], SUB, rpt).astype(BF16)
            gu = _dot(x, wgu_ref[0])
            de = gu.shape[1] // 2
            g = gu[:, :de]
            a = (g * jax.nn.sigmoid(g) * gu[:, de:]).astype(BF16)
            y = _dot(a, wd_ref[0])
            row = lax.broadcasted_iota(jnp.int32, (SUB, 1), 0) + sb * SUB
            keep = (row >= lo) & (row < hi)
            acc = jnp.where(first, 0.0, acc_ref[rows, :]) + jnp.where(keep, y, 0.0)
            acc_ref[rows, :] = acc
            _store_token_rows(o_ref.at[tok_rows], acc)


def _experts(item_blk, item_exp, item_lo, item_hi, xs, wgu, wd, rpt):
    a = xs.shape[0] // rpt
    d = rpt * LANES
    de = wd.shape[1]
    n_items = item_blk.shape[0]
    grid_spec = pltpu.PrefetchScalarGridSpec(
        num_scalar_prefetch=4,
        grid=(n_items,),
        in_specs=[
            pl.BlockSpec((BLK * rpt, LANES), lambda i, blk, ex, lo, hi: (blk[i], 0)),
            pl.BlockSpec((1, d, 2 * de), lambda i, blk, ex, lo, hi: (ex[i], 0, 0)),
            pl.BlockSpec((1, de, d), lambda i, blk, ex, lo, hi: (ex[i], 0, 0)),
        ],
        out_specs=pl.BlockSpec((BLK * rpt, LANES), lambda i, blk, ex, lo, hi: (blk[i], 0)),
        scratch_shapes=[pltpu.VMEM((BLK, d), F32)],
    )
    return pl.pallas_call(
        functools.partial(_experts_kernel, n_items=n_items, rpt=rpt),
        out_shape=jax.ShapeDtypeStruct((a * rpt, LANES), F32),
        grid_spec=grid_spec,
        compiler_params=pltpu.CompilerParams(
            dimension_semantics=("arbitrary",), vmem_limit_bytes=VMEM_LIMIT),
        name="experts",
    )(item_blk, item_exp, item_lo, item_hi, xs, wgu, wd)


def _work_items(counts, n_rows):
    n_blocks = n_rows // BLK
    n_items = n_blocks + N_EXPERTS - 1
    counts = counts.astype(jnp.int32)
    ends = jnp.cumsum(counts)
    starts = ends - counts
    first_blk = starts // BLK
    last_blk = jnp.maximum(ends - 1, 0) // BLK
    n_e = jnp.where(counts > 0, last_blk - first_blk + 1, 0)
    item_end = jnp.cumsum(n_e)
    item_start = item_end - n_e
    idx = jnp.arange(n_items, dtype=jnp.int32)
    total = item_end[-1]
    ex = jnp.minimum(jnp.sum((idx[:, None] >= item_end[None, :]).astype(jnp.int32), axis=1),
                     N_EXPERTS - 1)
    onehot = (ex[:, None] == jnp.arange(N_EXPERTS, dtype=jnp.int32)[None, :]).astype(jnp.int32)
    pick = lambda table: jnp.sum(onehot * table[None, :], axis=1)
    blk = pick(first_blk) + (idx - pick(item_start))
    lo = jnp.maximum(pick(starts), blk * BLK) - blk * BLK
    hi = jnp.minimum(pick(ends), (blk + 1) * BLK) - blk * BLK
    valid = idx < total
    last_ex = jnp.max(jnp.where(valid, ex, 0))
    blk = jnp.where(valid, blk, n_blocks - 1).astype(jnp.int32)
    ex = jnp.where(valid, ex, last_ex).astype(jnp.int32)
    lo = jnp.where(valid, lo, 0).astype(jnp.int32)
    hi = jnp.where(valid, hi, 0).astype(jnp.int32)
    return blk, ex, lo, hi


def _combine_kernel(dest_ref, dest_next_ref, ys_hbm, x1_ref, route_ref, mod_ref, gf_ref, o_ref,
                    ybuf, sems, *, tm, n_steps, final_norm, rpt):
    i = pl.program_id(0)
    slot = i % 2

    def gather(d_ref, slot_idx):
        def body(t, carry):
            for k in range(TOP_K):
                _token_copy(ys_hbm, d_ref[k, t], ybuf.at[slot_idx, k], t, sems.at[slot_idx],
                            rpt).start(priority=k)
            return carry
        lax.fori_loop(0, tm, body, 0, unroll=8)

    @pl.when(i == 0)
    def _():
        gather(dest_ref, 0)

    @pl.when(i + 1 < n_steps)
    def _():
        gather(dest_next_ref, 1 - slot)

    def drain(t, carry):
        for k in range(TOP_K):
            _token_copy(ys_hbm, 0, ybuf.at[slot, k], 0, sems.at[slot], rpt).wait()
        return carry

    lax.fori_loop(0, tm, drain, 0, unroll=8)

    r = route_ref[...]
    y = (r[:, 4:5] * _load_token_rows(ybuf.at[slot, 0], tm, rpt)
         + r[:, 5:6] * _load_token_rows(ybuf.at[slot, 1], tm, rpt))
    mod = mod_ref[0]
    x2 = x1_ref[...] + mod[5:6] * y
    o_ref[...] = _rms(x2, gf_ref[...]) if final_norm else x2


def _combine(dest, ys, x1, route, mod, gf, seq, final_norm):
    t, d = x1.shape
    tm = TM_COMB
    n_steps = t // tm
    per_b = seq // tm
    rpt = d // LANES
    return pl.pallas_call(
        functools.partial(_combine_kernel, tm=tm, n_steps=n_steps, final_norm=final_norm, rpt=rpt),
        out_shape=jax.ShapeDtypeStruct((t, d), F32),
        grid=(n_steps,),
        in_specs=[
            pl.BlockSpec((SUBLANES, tm), lambda i: (0, i), memory_space=pltpu.SMEM),
            pl.BlockSpec((SUBLANES, tm), lambda i: (0, jnp.minimum(i + 1, n_steps - 1)),
                         memory_space=pltpu.SMEM),
            pl.BlockSpec(memory_space=pl.ANY),
            pl.BlockSpec((tm, d), lambda i: (i, 0)),
            pl.BlockSpec((tm, LANES), lambda i: (i, 0)),
            pl.BlockSpec((1, N_MOD, d), lambda i: (i // per_b, 0, 0)),
            pl.BlockSpec((1, d), lambda i: (0, 0)),
        ],
        out_specs=pl.BlockSpec((tm, d), lambda i: (i, 0)),
        scratch_shapes=[pltpu.VMEM((2, TOP_K, tm * rpt, LANES), F32),
                        pltpu.SemaphoreType.DMA((2,))],
        compiler_params=pltpu.CompilerParams(
            dimension_semantics=("arbitrary",), vmem_limit_bytes=VMEM_LIMIT),
        name="combine",
    )(dest, dest, ys, x1, route, mod, gf)


def _prep_in_weights(w_in, q_rank, kv_rank, conv_dim):
    o1 = q_rank
    o2 = o1 + kv_rank
    o3 = o2 + D_ROPE
    d = w_in.shape[0]
    w_kr = w_in[:, o2:o3]
    half = D_ROPE // 2
    w_kr_rot = jnp.concatenate([-w_kr[:, half:], w_kr[:, :half]], axis=1)
    zl = jnp.zeros((d, D_NOPE), w_in.dtype)
    zr = jnp.zeros((d, LANES - D_NOPE - D_ROPE), w_in.dtype)
    return jnp.concatenate(
        [w_in[:, :o2], zl, w_kr, zr, zl, w_kr_rot, zr, w_in[:, o3:]], axis=1).astype(BF16)


def _prep_q_weights(w_uq):
    r = w_uq.shape[0]
    w = w_uq.reshape(r, HEADS, D_NOPE + D_ROPE)
    nope, rope = w[..., :D_NOPE], w[..., D_NOPE:]
    half = D_ROPE // 2
    rot = jnp.concatenate([-rope[..., half:], rope[..., :half]], axis=-1)
    zr = jnp.zeros((r, HEADS, HEAD_PAD - D_NOPE - D_ROPE), w_uq.dtype)
    w1 = jnp.concatenate([nope, rope, zr], axis=-1).reshape(r, HEADS * HEAD_PAD)
    w2 = jnp.concatenate([jnp.zeros_like(nope), rot, zr], axis=-1).reshape(r, HEADS * HEAD_PAD)
    return w1.astype(BF16), w2.astype(BF16)


def _prep_k_weights(w_uk):
    r = w_uk.shape[0]
    w = w_uk.reshape(r, HEADS, D_NOPE)
    zr = jnp.zeros((r, HEADS, HEAD_PAD - D_NOPE), w_uk.dtype)
    return jnp.concatenate([w, zr], axis=-1).reshape(r, HEADS * HEAD_PAD).astype(BF16)


def _rope_freqs():
    inv_freq = 1.0 / (ROPE_THETA ** (jnp.arange(0, D_ROPE, 2, dtype=F32) / D_ROPE))
    return inv_freq.reshape(D_ROPE // 2, 1)


def kernel(x, c, positions, w_ada, b_ada, norm1_g, w_in, q_norm_g, kv_norm_g, w_uq, w_uk, w_uv,
           conv_w, conv_b, attn_out_g, conv_out_g, w_out, norm2_g, w_router_group, b_router_group,
           w_router_expert, b_router_expert, w_exp_gate, w_exp_up, w_exp_down, final_g):
    bsz, seq, d = x.shape
    depth = w_ada.shape[0]
    q_rank = q_norm_g.shape[-1]
    kv_rank = kv_norm_g.shape[-1]
    conv_dim = conv_w.shape[-1]
    t = bsz * seq
    assert seq % max(TM_PROJ, TQ, TM_OUT, TM_COMB) == 0 and TQ % TK == 0 and TM_PROJ % TK == 0
    assert CHUNK & (CHUNK - 1) == 0 and TQ % CHUNK == 0
    assert t % max(TM_DEST, TM_DISP, TM_COMB) == 0 and (t * TOP_K) % BLK == 0
    assert w_uq.shape[-1] == HEADS * (D_NOPE + D_ROPE) and w_uv.shape[-1] == HEADS * D_V

    pos = positions.reshape(bsz, 1, seq)
    invf = _rope_freqs()
    row = lambda v: v.reshape(1, -1)
    for l in range(depth):
        mod = _adaln(c, w_ada[l], b_ada[l]).reshape(bsz, N_MOD, d)
        win = _prep_in_weights(w_in[l], q_rank, kv_rank, conv_dim)
        wq1, wq2 = _prep_q_weights(w_uq[l])
        wk = _prep_k_weights(w_uk[l])
        wvt = w_uv[l].T.astype(BF16)
        q, k, vt, conv = _inproj(x, mod, pos, row(norm1_g[l]), win, row(q_norm_g[l]),
                                 row(kv_norm_g[l]), wq1, wq2, wk, wvt, invf, conv_w[l],
                                 row(conv_b[l]), row(conv_out_g[l]))
        attn = _attention(q, k, vt)

        wr = jnp.concatenate(
            [w_router_expert[l], w_router_group[l],
             jnp.zeros((d, LANES - N_EXPERTS - N_GROUPS), F32)], axis=1).astype(BF16)
        br = jnp.concatenate(
            [b_router_expert[l], b_router_group[l],
             jnp.zeros((LANES - N_EXPERTS - N_GROUPS,), F32)]).reshape(1, LANES)
        x1, h2, route, rt, cnt = _outproj(x, attn, conv, mod, row(attn_out_g[l]),
                                      w_out[l].astype(BF16), row(norm2_g[l]), wr, br)
        route = route.reshape(t, LANES)
        counts = cnt[0, :N_EXPERTS].astype(jnp.int32)
        starts = jnp.cumsum(counts) - counts
        rpt = d // LANES
        dest = _dest(starts, rt)
        xs = _dispatch(dest, h2, rpt)
        items = _work_items(counts, t * TOP_K)
        wgu = jnp.concatenate([w_exp_gate[l], w_exp_up[l]], axis=-1).astype(BF16)
        ys = _experts(*items, xs, wgu, w_exp_down[l].astype(BF16), rpt)
        out = _combine(dest, ys, x1.reshape(t, d), route, mod, row(final_g), seq, l == depth - 1)
        x = out.reshape(bsz, seq, d)
    return x
```

```python
import functools
import math

import jax
import jax.numpy as jnp
from jax import lax
from jax.experimental import pallas as pl
from jax.experimental.pallas import tpu as pltpu

F32 = jnp.float32
BF16 = jnp.bfloat16

CHUNK = 64
HEADS = 8
D_NOPE = 64
D_ROPE = 32
D_V = 64
ROPE_THETA = 10000.0
CONV_WIDTH = 3
N_GROUPS = 4
EXPERTS_PER_GROUP = 8
N_EXPERTS = N_GROUPS * EXPERTS_PER_GROUP
TOP_K = 2
N_MOD = 6
RMS_EPS = 1e-6
NEG_INF = -1e30
LOG2_E = math.log2(math.e)

LANES = 128
SUBLANES = 8
HEAD_PAD = LANES
DEN_ROWS = 16
VMEM_LIMIT = 56 * 1024 * 1024

TM_PROJ = 512
TQ = 512
TK = 256
TM_OUT = 1024
TM_DEST = 2048
TM_DISP = 256
BLK = 256
SUB = 256
TM_COMB = 256
TN_ADA = 512
GATHER_AHEAD = 2
N_SLOTS = GATHER_AHEAD + 1


def _rms(x, g):
    return x * lax.rsqrt(jnp.mean(x * x, axis=-1, keepdims=True) + RMS_EPS) * g


def _dot(a, b):
    return jnp.dot(a, b, preferred_element_type=F32)


def _dot_nt(a, b):
    return lax.dot_general(a, b, (((1,), (1,)), ((), ())), preferred_element_type=F32)


def _store_token_rows(ref, x):
    n, d = x.shape
    r = d // LANES
    for j in range(r):
        ref[pl.ds(j, n, stride=r), :] = x[:, j * LANES:(j + 1) * LANES]


def _load_token_rows(ref, n, r):
    return jnp.concatenate([ref[pl.ds(j, n, stride=r), :] for j in range(r)], axis=1)


def _split_bf16(x):
    hi = x.astype(BF16)
    lo = (x - hi.astype(F32)).astype(BF16)
    return hi, lo


def _adaln_kernel(c_ref, w_ref, b_ref, o_ref):
    c = c_ref[...]
    ca = c * jax.nn.sigmoid(c)
    c_hi, c_lo = _split_bf16(ca)
    w_hi, w_lo = _split_bf16(w_ref[...])
    acc = _dot(c_hi, w_hi) + _dot(c_hi, w_lo) + _dot(c_lo, w_hi)
    o_ref[...] = acc + b_ref[...]


def _adaln(c, w, b):
    bsz, d = c.shape
    n = w.shape[1]
    return pl.pallas_call(
        _adaln_kernel,
        out_shape=jax.ShapeDtypeStruct((bsz, n), F32),
        grid=(n // TN_ADA,),
        in_specs=[
            pl.BlockSpec((bsz, d), lambda j: (0, 0)),
            pl.BlockSpec((d, TN_ADA), lambda j: (0, j)),
            pl.BlockSpec((1, TN_ADA), lambda j: (0, j)),
        ],
        out_specs=pl.BlockSpec((bsz, TN_ADA), lambda j: (0, j)),
        compiler_params=pltpu.CompilerParams(
            dimension_semantics=("arbitrary",), vmem_limit_bytes=VMEM_LIMIT),
        name="adaln",
    )(c, w, b.reshape(1, n))


def _inproj_kernel(x_ref, mod_ref, pos_ref, g1_ref, win_ref, gq_ref, gkv_ref, wq1_ref, wq2_ref,
                   wk_ref, wvt_ref, invf_ref, cw_ref, cb_ref, gc_ref,
                   q_out, k_out, vt_out, conv_out, zbuf, *, tm, q_rank, kv_rank, conv_dim):
    si = pl.program_id(1)
    x = x_ref[0]
    mod = mod_ref[0]
    h = (_rms(x, g1_ref[...]) * (1.0 + mod[1:2]) + mod[0:1]).astype(BF16)
    proj = _dot(h, win_ref[...])

    o_kv = q_rank
    o_kr = o_kv + kv_rank
    o_u = o_kr + 2 * LANES
    q_c = proj[:, 0:q_rank]
    kv_c = proj[:, o_kv:o_kv + kv_rank]
    kr = proj[:, o_kr:o_kr + LANES]
    kr_rot = proj[:, o_kr + LANES:o_kr + 2 * LANES]
    u = proj[:, o_u:o_u + conv_dim]
    b_gate = proj[:, o_u + conv_dim:o_u + 2 * conv_dim]
    c_gate = proj[:, o_u + 2 * conv_dim:o_u + 3 * conv_dim]

    ang_t = invf_ref[...] * pos_ref[0].astype(F32)

    def head_lanes(tab_t):
        zl = jnp.zeros((D_NOPE, tm), F32)
        zr = jnp.zeros((LANES - D_NOPE - D_ROPE, tm), F32)
        return jnp.concatenate([zl, tab_t, tab_t, zr], axis=0).T

    cos_t = head_lanes(jnp.cos(ang_t))
    sin_t = head_lanes(jnp.sin(ang_t))
    lane = lax.broadcasted_iota(jnp.int32, (1, LANES), 1)
    q_scale = LOG2_E / math.sqrt(D_NOPE + D_ROPE)
    q_cos = jnp.where(lane < D_NOPE, q_scale, cos_t * q_scale)
    q_sin = sin_t * q_scale

    qn = _rms(q_c, gq_ref[...]).astype(BF16)
    qa = _dot(qn, wq1_ref[...])
    qb = _dot(qn, wq2_ref[...])
    kvn = _rms(kv_c, gkv_ref[...]).astype(BF16)
    kn = _dot(kvn, wk_ref[...])
    k_rope = kr * cos_t + kr_rot * sin_t
    for hd in range(HEADS):
        sl = slice(hd * HEAD_PAD, (hd + 1) * HEAD_PAD)
        q_out[0, hd] = (qa[:, sl] * q_cos + qb[:, sl] * q_sin).astype(BF16)
        k_out[0, hd] = (kn[:, sl] + k_rope).astype(BF16)

    vt = _dot_nt(wvt_ref[...], kvn)
    for j in range(tm // TK):
        vt_out[0, j] = vt[:, j * TK:(j + 1) * TK].astype(BF16)

    z = c_gate * u

    @pl.when(si == 0)
    def _():
        zbuf[0:SUBLANES, :] = jnp.zeros((SUBLANES, conv_dim), F32)

    zbuf[SUBLANES:SUBLANES + tm, :] = z
    z1 = zbuf[SUBLANES - 1:SUBLANES - 1 + tm, :]
    z2 = zbuf[SUBLANES - 2:SUBLANES - 2 + tm, :]
    cw = cw_ref[...]
    zc = cw[0:1] * z2 + cw[1:2] * z1 + cw[2:3] * z + cb_ref[...]
    conv = b_gate * zc
    conv_out[0] = _rms(conv, gc_ref[...]).astype(BF16)
    zbuf[0:SUBLANES, :] = zbuf[tm:tm + SUBLANES, :]


def _inproj(x, mod, pos, g1, win, gq, gkv, wq1, wq2, wk, wvt, invf, cw, cb, gc):
    bsz, seq, d = x.shape
    tm = TM_PROJ
    q_rank, kv_rank = wq1.shape[0], wk.shape[0]
    conv_dim = cw.shape[1]
    hv = wvt.shape[0]
    n_s = seq // tm
    const2 = lambda b, s: (0, 0)
    kern = functools.partial(_inproj_kernel, tm=tm, q_rank=q_rank, kv_rank=kv_rank,
                             conv_dim=conv_dim)
    return pl.pallas_call(
        kern,
        out_shape=(
            jax.ShapeDtypeStruct((bsz, HEADS, seq, HEAD_PAD), BF16),
            jax.ShapeDtypeStruct((bsz, HEADS, seq, HEAD_PAD), BF16),
            jax.ShapeDtypeStruct((bsz, seq // TK, hv, TK), BF16),
            jax.ShapeDtypeStruct((bsz, seq, conv_dim), BF16),
        ),
        grid=(bsz, n_s),
        in_specs=[
            pl.BlockSpec((1, tm, d), lambda b, s: (b, s, 0)),
            pl.BlockSpec((1, N_MOD, d), lambda b, s: (b, 0, 0)),
            pl.BlockSpec((1, 1, tm), lambda b, s: (b, 0, s)),
            pl.BlockSpec((1, d), const2),
            pl.BlockSpec(win.shape, const2),
            pl.BlockSpec((1, q_rank), const2),
            pl.BlockSpec((1, kv_rank), const2),
            pl.BlockSpec(wq1.shape, const2),
            pl.BlockSpec(wq2.shape, const2),
            pl.BlockSpec(wk.shape, const2),
            pl.BlockSpec(wvt.shape, const2),
            pl.BlockSpec(invf.shape, const2),
            pl.BlockSpec(cw.shape, const2),
            pl.BlockSpec((1, conv_dim), const2),
            pl.BlockSpec((1, conv_dim), const2),
        ],
        out_specs=(
            pl.BlockSpec((1, HEADS, tm, HEAD_PAD), lambda b, s: (b, 0, s, 0)),
            pl.BlockSpec((1, HEADS, tm, HEAD_PAD), lambda b, s: (b, 0, s, 0)),
            pl.BlockSpec((1, tm // TK, hv, TK), lambda b, s: (b, s, 0, 0)),
            pl.BlockSpec((1, tm, conv_dim), lambda b, s: (b, s, 0)),
        ),
        scratch_shapes=[pltpu.VMEM((tm + 2 * SUBLANES, conv_dim), F32)],
        compiler_params=pltpu.CompilerParams(
            dimension_semantics=("arbitrary", "arbitrary"), vmem_limit_bytes=VMEM_LIMIT),
        name="inproj",
    )(x, mod, pos, g1, win, gq, gkv, wq1, wq2, wk, wvt, invf, cw, cb, gc)


def _attn_kernel(q_ref, k_ref, vt_ref, o_ref, *, n_q):
    row = lax.broadcasted_iota(jnp.int32, (TQ, TQ), 0)
    col = lax.broadcasted_iota(jnp.int32, (TQ, TQ), 1)
    shift = CHUNK.bit_length() - 1
    diag_ok = (row >> shift) <= (col >> shift)
    vt_per_step = TQ // TK

    ones_rows = jnp.ones((DEN_ROWS, TQ), BF16)

    def scores(hh, qi, j):
        return _dot_nt(k_ref[0, hh, j * TQ:(j + 1) * TQ, :], q_ref[0, hh, qi * TQ:(qi + 1) * TQ, :])

    def softmax_pv(hh, j, s_t, state, masked):
        m, acc = state
        if masked:
            s_t = jnp.where(diag_ok, s_t, NEG_INF)
        m_new = jnp.maximum(m, jnp.max(s_t, axis=0, keepdims=True))
        p = jnp.exp2(s_t - m_new).astype(BF16)
        alpha = jnp.exp2(m - m_new)
        v_t = jnp.concatenate(
            [vt_ref[0, j * vt_per_step + i, hh * D_V:(hh + 1) * D_V, :]
             for i in range(vt_per_step)], axis=1)
        v_aug = jnp.concatenate([v_t, ones_rows], axis=0)
        acc_new = alpha * acc + _dot(v_aug, p)
        return m_new, acc_new

    steps = [(qi, j) for qi in range(n_q) for j in range(qi + 1)]
    init = (jnp.full((1, TQ), NEG_INF, F32), jnp.zeros((D_V + DEN_ROWS, TQ), F32))
    s_cur = tuple(scores(hh, *steps[0]) for hh in range(2))
    states = None
    for idx, (qi, j) in enumerate(steps):
        s_next = None
        if idx + 1 < len(steps):
            s_next = tuple(scores(hh, *steps[idx + 1]) for hh in range(2))
        if j == 0:
            states = (init, init)
        states = tuple(softmax_pv(hh, j, s_cur[hh], states[hh], j == qi) for hh in range(2))
        if j == qi:
            o_t = jnp.concatenate([acc[:D_V] / acc[D_V:D_V + 1] for (_, acc) in states],
                                  axis=0)
            o_ref[0, qi * TQ:(qi + 1) * TQ, :] = o_t.T.astype(BF16)
        s_cur = s_next


def _attention(q, k, vt):
    bsz, _, seq, _ = q.shape
    n_pair = HEADS // 2
    return pl.pallas_call(
        functools.partial(_attn_kernel, n_q=seq // TQ),
        out_shape=jax.ShapeDtypeStruct((bsz, seq, HEADS * D_V), BF16),
        grid=(bsz, n_pair),
        in_specs=[
            pl.BlockSpec((1, 2, seq, HEAD_PAD), lambda b, p: (b, p, 0, 0)),
            pl.BlockSpec((1, 2, seq, HEAD_PAD), lambda b, p: (b, p, 0, 0)),
            pl.BlockSpec((1, seq // TK, 2 * D_V, TK), lambda b, p: (b, 0, p, 0)),
        ],
        out_specs=pl.BlockSpec((1, seq, 2 * D_V), lambda b, p: (b, 0, p)),
        compiler_params=pltpu.CompilerParams(
            dimension_semantics=("arbitrary", "arbitrary"), vmem_limit_bytes=VMEM_LIMIT),
        name="attn",
    )(q, k, vt)


def _outproj_kernel(x_ref, attn_ref, conv_ref, mod_ref, ga_ref, wout_ref, g2_ref, wr_ref, br_ref,
                    tri_ref, x1_out, h2_out, route_out, rt_out, cnt_out, base_ref):
    first = (pl.program_id(0) == 0) & (pl.program_id(1) == 0)

    @pl.when(first)
    def _():
        base_ref[...] = jnp.zeros(base_ref.shape, F32)

    mod = mod_ref[0]
    an = _rms(attn_ref[0].astype(F32), ga_ref[...]).astype(BF16)
    mixed = jnp.concatenate([an, conv_ref[0]], axis=-1)
    y = _dot(mixed, wout_ref[...])
    x1 = x_ref[0] + mod[2:3] * y
    x1_out[0] = x1
    h2 = _rms(x1, g2_ref[...]) * (1.0 + mod[4:5]) + mod[3:4]
    _store_token_rows(h2_out, h2)
    logits = _dot(h2.astype(BF16), wr_ref[...]) + br_ref[...]

    tm = logits.shape[0]
    lane = lax.broadcasted_iota(jnp.int32, (tm, LANES), 1).astype(F32)
    big = float(LANES)
    is_grp = (lane >= N_EXPERTS) & (lane < N_EXPERTS + N_GROUPS)
    gl = jnp.where(is_grp, logits, -jnp.inf)
    gmax = jnp.max(gl, axis=-1, keepdims=True)
    gidx = jnp.min(jnp.where(gl == gmax, lane, big), axis=-1, keepdims=True) - N_EXPERTS
    p_group = 1.0 / jnp.sum(jnp.where(is_grp, jnp.exp(logits - gmax), 0.0), axis=-1, keepdims=True)
    lo = gidx * EXPERTS_PER_GROUP
    in_grp = (lane >= lo) & (lane < lo + EXPERTS_PER_GROUP)
    el = jnp.where(in_grp, logits, -jnp.inf)
    v1 = jnp.max(el, axis=-1, keepdims=True)
    i1 = jnp.min(jnp.where(el == v1, lane, big), axis=-1, keepdims=True)
    el2 = jnp.where(lane == i1, -jnp.inf, el)
    v2 = jnp.max(el2, axis=-1, keepdims=True)
    i2 = jnp.min(jnp.where(el2 == v2, lane, big), axis=-1, keepdims=True)
    t = jnp.exp(v2 - v1)
    w1 = p_group / (1.0 + t)
    w2 = p_group * t / (1.0 + t)

    oh1 = lane == i1
    oh2 = lane == i2
    cnt = jnp.where(oh1 | oh2, 1.0, 0.0)
    prefix = _dot(tri_ref[...], cnt.astype(BF16)) + base_ref[0:1, :]
    r1 = jnp.sum(jnp.where(oh1, prefix, 0.0), axis=-1, keepdims=True)
    r2 = jnp.sum(jnp.where(oh2, prefix, 0.0), axis=-1, keepdims=True)
    new_base = base_ref[...] + jnp.sum(cnt, axis=0, keepdims=True)
    base_ref[...] = new_base
    cnt_out[...] = new_base

    route = jnp.where(lane == 0.0, i1, jnp.where(lane == 1.0, i2, jnp.where(
        lane == 2.0, r1, jnp.where(lane == 3.0, r2, jnp.where(
            lane == 4.0, w1, jnp.where(lane == 5.0, w2, 0.0))))))
    route_out[0] = route
    rt_out[...] = route.T[0:SUBLANES, :].astype(jnp.int32)


def _outproj(x, attn, conv, mod, ga, wout, g2, wr, br):
    bsz, seq, d = x.shape
    tm = TM_OUT
    aw = attn.shape[-1]
    cwd = conv.shape[-1]
    n_s = seq // tm
    rpt = d // LANES
    tri = (lax.broadcasted_iota(jnp.int32, (tm, tm), 1)
           < lax.broadcasted_iota(jnp.int32, (tm, tm), 0)).astype(BF16)
    const2 = lambda b, s: (0, 0)
    tile3 = lambda b, s: (b, s, 0)
    return pl.pallas_call(
        _outproj_kernel,
        out_shape=(
            jax.ShapeDtypeStruct((bsz, seq, d), F32),
            jax.ShapeDtypeStruct((bsz * seq * rpt, LANES), F32),
            jax.ShapeDtypeStruct((bsz, seq, LANES), F32),
            jax.ShapeDtypeStruct((SUBLANES, bsz * seq), jnp.int32),
            jax.ShapeDtypeStruct((SUBLANES, LANES), F32),
        ),
        grid=(bsz, n_s),
        in_specs=[
            pl.BlockSpec((1, tm, d), tile3),
            pl.BlockSpec((1, tm, aw), tile3),
            pl.BlockSpec((1, tm, cwd), tile3),
            pl.BlockSpec((1, N_MOD, d), lambda b, s: (b, 0, 0)),
            pl.BlockSpec((1, aw), const2),
            pl.BlockSpec(wout.shape, const2),
            pl.BlockSpec((1, d), const2),
            pl.BlockSpec(wr.shape, const2),
            pl.BlockSpec((1, LANES), const2),
            pl.BlockSpec((tm, tm), const2),
        ],
        out_specs=(
            pl.BlockSpec((1, tm, d), tile3),
            pl.BlockSpec((tm * rpt, LANES), lambda b, s: (b * n_s + s, 0)),
            pl.BlockSpec((1, tm, LANES), tile3),
            pl.BlockSpec((SUBLANES, tm), lambda b, s: (0, b * n_s + s)),
            pl.BlockSpec((SUBLANES, LANES), const2),
        ),
        scratch_shapes=[pltpu.VMEM((SUBLANES, LANES), F32)],
        compiler_params=pltpu.CompilerParams(
            dimension_semantics=("arbitrary", "arbitrary"), vmem_limit_bytes=VMEM_LIMIT),
        name="outproj",
    )(x, attn, conv, mod, ga, wout, g2, wr, br, tri)


def _dest_kernel(starts_ref, rt_ref, dest_out):
    rt = rt_ref[...]
    ex = rt[0:TOP_K, :]
    start = jnp.zeros(ex.shape, jnp.int32)
    for e in range(N_EXPERTS):
        start = jnp.where(ex == e, starts_ref[e], start)
    dest = start + rt[TOP_K:2 * TOP_K, :]
    dest_out[...] = jnp.concatenate(
        [dest, jnp.zeros((SUBLANES - TOP_K, dest.shape[1]), jnp.int32)], axis=0)


def _dest(starts, rt):
    t = rt.shape[1]
    tm = TM_DEST
    grid_spec = pltpu.PrefetchScalarGridSpec(
        num_scalar_prefetch=1,
        grid=(t // tm,),
        in_specs=[pl.BlockSpec((SUBLANES, tm), lambda i, st: (0, i))],
        out_specs=pl.BlockSpec((SUBLANES, tm), lambda i, st: (0, i)),
    )
    return pl.pallas_call(
        _dest_kernel,
        out_shape=jax.ShapeDtypeStruct((SUBLANES, t), jnp.int32),
        grid_spec=grid_spec,
        compiler_params=pltpu.CompilerParams(
            dimension_semantics=("arbitrary",), vmem_limit_bytes=VMEM_LIMIT),
        name="dest",
    )(starts, rt)


def _token_copy(src, src_tok, dst, dst_tok, sem, rpt):
    return pltpu.make_async_copy(src.at[pl.ds(src_tok * rpt, rpt)], dst.at[pl.ds(dst_tok * rpt, rpt)],
                                 sem)


def _dispatch_kernel(dest_ref, h_ref, xs_hbm, sem, *, tm, rpt):
    for t in range(tm):
        for k in range(TOP_K):
            _token_copy(h_ref, t, xs_hbm, dest_ref[k, t], sem, rpt).start(priority=k)

    def drain(t, carry):
        for k in range(TOP_K):
            _token_copy(h_ref, 0, xs_hbm, 0, sem, rpt).wait()
        return carry

    lax.fori_loop(0, tm, drain, 0, unroll=8)


def _dispatch(dest, h, rpt):
    t = h.shape[0] // rpt
    tm = TM_DISP
    return pl.pallas_call(
        functools.partial(_dispatch_kernel, tm=tm, rpt=rpt),
        out_shape=jax.ShapeDtypeStruct((t * TOP_K * rpt, LANES), h.dtype),
        grid=(t // tm,),
        in_specs=[
            pl.BlockSpec((SUBLANES, tm), lambda i: (0, i), memory_space=pltpu.SMEM),
            pl.BlockSpec((tm * rpt, LANES), lambda i: (i, 0)),
        ],
        out_specs=pl.BlockSpec(memory_space=pl.ANY),
        scratch_shapes=[pltpu.SemaphoreType.DMA],
        compiler_params=pltpu.CompilerParams(
            dimension_semantics=("arbitrary",), vmem_limit_bytes=VMEM_LIMIT),
        name="dispatch",
    )(dest, h)


def _experts_kernel(blk_ref, exp_ref, lo_ref, hi_ref, x_ref, wgu_ref, wd_ref, o_ref, acc_ref,
                    *, n_items, rpt):
    i = pl.program_id(0)
    lo = lo_ref[i]
    hi = hi_ref[i]
    blk = blk_ref[i]
    first = (i == 0) | (blk != blk_ref[jnp.maximum(i - 1, 0)])

    @pl.when(i == 0)
    def _():
        acc_ref[...] = jnp.zeros(acc_ref.shape, F32)

    @pl.when(hi > lo)
    def _():
        for sb in range(BLK // SUB):
            rows = pl.ds(sb * SUB, SUB)
            tok_rows = pl.ds(sb * SUB * rpt, SUB * rpt)
            x = _load_token_rows(x_ref.at[tok_rows], SUB, rpt).astype(BF16)
            gu = _dot(x, wgu_ref[0])
            de = gu.shape[1] // 2
            g = gu[:, :de]
            a = (g * jax.nn.sigmoid(g) * gu[:, de:]).astype(BF16)
            y = _dot(a, wd_ref[0])
            row = lax.broadcasted_iota(jnp.int32, (SUB, 1), 0) + sb * SUB
            keep = (row >= lo) & (row < hi)
            acc = jnp.where(first, 0.0, acc_ref[rows, :]) + jnp.where(keep, y, 0.0)
            acc_ref[rows, :] = acc
            _store_token_rows(o_ref.at[tok_rows], acc)


def _experts(item_blk, item_exp, item_lo, item_hi, xs, wgu, wd, rpt):
    a = xs.shape[0] // rpt
    d = rpt * LANES
    de = wd.shape[1]
    n_items = item_blk.shape[0]
    grid_spec = pltpu.PrefetchScalarGridSpec(
        num_scalar_prefetch=4,
        grid=(n_items,),
        in_specs=[
            pl.BlockSpec((BLK * rpt, LANES), lambda i, blk, ex, lo, hi: (blk[i], 0)),
            pl.BlockSpec((1, d, 2 * de), lambda i, blk, ex, lo, hi: (ex[i], 0, 0)),
            pl.BlockSpec((1, de, d), lambda i, blk, ex, lo, hi: (ex[i], 0, 0)),
        ],
        out_specs=pl.BlockSpec((BLK * rpt, LANES), lambda i, blk, ex, lo, hi: (blk[i], 0)),
        scratch_shapes=[pltpu.VMEM((BLK, d), F32)],
    )
    return pl.pallas_call(
        functools.partial(_experts_kernel, n_items=n_items, rpt=rpt),
        out_shape=jax.ShapeDtypeStruct((a * rpt, LANES), F32),
        grid_spec=grid_spec,
        compiler_params=pltpu.CompilerParams(
            dimension_semantics=("arbitrary",), vmem_limit_bytes=VMEM_LIMIT),
        name="experts",
    )(item_blk, item_exp, item_lo, item_hi, xs, wgu, wd)


def _work_items(counts, n_rows):
    n_blocks = n_rows // BLK
    n_items = n_blocks + N_EXPERTS - 1
    counts = counts.astype(jnp.int32)
    ends = jnp.cumsum(counts)
    starts = ends - counts
    first_blk = starts // BLK
    last_blk = jnp.maximum(ends - 1, 0) // BLK
    n_e = jnp.where(counts > 0, last_blk - first_blk + 1, 0)
    item_end = jnp.cumsum(n_e)
    item_start = item_end - n_e
    idx = jnp.arange(n_items, dtype=jnp.int32)
    total = item_end[-1]
    ex = jnp.minimum(jnp.sum((idx[:, None] >= item_end[None, :]).astype(jnp.int32), axis=1),
                     N_EXPERTS - 1)
    onehot = (ex[:, None] == jnp.arange(N_EXPERTS, dtype=jnp.int32)[None, :]).astype(jnp.int32)
    pick = lambda table: jnp.sum(onehot * table[None, :], axis=1)
    blk = pick(first_blk) + (idx - pick(item_start))
    lo = jnp.maximum(pick(starts), blk * BLK) - blk * BLK
    hi = jnp.minimum(pick(ends), (blk + 1) * BLK) - blk * BLK
    valid = idx < total
    last_ex = jnp.max(jnp.where(valid, ex, 0))
    blk = jnp.where(valid, blk, n_blocks - 1).astype(jnp.int32)
    ex = jnp.where(valid, ex, last_ex).astype(jnp.int32)
    lo = jnp.where(valid, lo, 0).astype(jnp.int32)
    hi = jnp.where(valid, hi, 0).astype(jnp.int32)
    return blk, ex, lo, hi


def _combine_kernel(dest_ref, dest_one_ref, dest_ahead_ref, ys_hbm, x1_ref, route_ref, mod_ref,
                    gf_ref, o_ref, ybuf, sems, *, tm, final_norm, rpt):
    i = pl.program_id(0)
    slot = i % N_SLOTS

    def gather(d_ref, slot_idx):
        for t in range(tm):
            for k in range(TOP_K):
                _token_copy(ys_hbm, d_ref[k, t], ybuf.at[slot_idx, k], t, sems.at[slot_idx],
                            rpt).start(priority=k)

    def drain(slot_idx):
        def body(t, carry):
            for k in range(TOP_K):
                _token_copy(ys_hbm, 0, ybuf.at[slot_idx, k], 0, sems.at[slot_idx], rpt).wait()
            return carry
        lax.fori_loop(0, tm, body, 0, unroll=8)

    @pl.when(i == 0)
    def _():
        gather(dest_ref, 0)
        gather(dest_one_ref, 1)

    drain(slot)
    r = route_ref[...]
    y = (r[:, 4:5] * _load_token_rows(ybuf.at[slot, 0], tm, rpt)
         + r[:, 5:6] * _load_token_rows(ybuf.at[slot, 1], tm, rpt))
    mod = mod_ref[0]
    x2 = x1_ref[...] + mod[5:6] * y
    o_ref[...] = _rms(x2, gf_ref[...]) if final_norm else x2

    gather(dest_ahead_ref, (i + GATHER_AHEAD) % N_SLOTS)

    @pl.when(i == pl.num_programs(0) - 1)
    def _():
        for extra in range(1, N_SLOTS):
            drain((i + extra) % N_SLOTS)


def _combine(dest, ys, x1, route, mod, gf, seq, final_norm):
    t, d = x1.shape
    tm = TM_COMB
    n_steps = t // tm
    per_b = seq // tm
    rpt = d // LANES
    return pl.pallas_call(
        functools.partial(_combine_kernel, tm=tm, final_norm=final_norm, rpt=rpt),
        out_shape=jax.ShapeDtypeStruct((t, d), F32),
        grid=(n_steps,),
        in_specs=[
            pl.BlockSpec((SUBLANES, tm), lambda i: (0, i), memory_space=pltpu.SMEM),
            pl.BlockSpec((SUBLANES, tm), lambda i: (0, min(1, n_steps - 1)),
                         memory_space=pltpu.SMEM),
            pl.BlockSpec((SUBLANES, tm), lambda i: (0, jnp.minimum(i + GATHER_AHEAD, n_steps - 1)),
                         memory_space=pltpu.SMEM),
            pl.BlockSpec(memory_space=pl.ANY),
            pl.BlockSpec((tm, d), lambda i: (i, 0)),
            pl.BlockSpec((tm, LANES), lambda i: (i, 0)),
            pl.BlockSpec((1, N_MOD, d), lambda i: (i // per_b, 0, 0)),
            pl.BlockSpec((1, d), lambda i: (0, 0)),
        ],
        out_specs=pl.BlockSpec((tm, d), lambda i: (i, 0)),
        scratch_shapes=[pltpu.VMEM((N_SLOTS, TOP_K, tm * rpt, LANES), F32),
                        pltpu.SemaphoreType.DMA((N_SLOTS,))],
        compiler_params=pltpu.CompilerParams(
            dimension_semantics=("arbitrary",), vmem_limit_bytes=VMEM_LIMIT),
        name="combine",
    )(dest, dest, dest, ys, x1, route, mod, gf)


def _prep_in_weights(w_in, q_rank, kv_rank, conv_dim):
    o1 = q_rank
    o2 = o1 + kv_rank
    o3 = o2 + D_ROPE
    d = w_in.shape[0]
    w_kr = w_in[:, o2:o3]
    half = D_ROPE // 2
    w_kr_rot = jnp.concatenate([-w_kr[:, half:], w_kr[:, :half]], axis=1)
    zl = jnp.zeros((d, D_NOPE), w_in.dtype)
    zr = jnp.zeros((d, LANES - D_NOPE - D_ROPE), w_in.dtype)
    return jnp.concatenate(
        [w_in[:, :o2], zl, w_kr, zr, zl, w_kr_rot, zr, w_in[:, o3:]], axis=1).astype(BF16)


def _prep_q_weights(w_uq):
    r = w_uq.shape[0]
    w = w_uq.reshape(r, HEADS, D_NOPE + D_ROPE)
    nope, rope = w[..., :D_NOPE], w[..., D_NOPE:]
    half = D_ROPE // 2
    rot = jnp.concatenate([-rope[..., half:], rope[..., :half]], axis=-1)
    zr = jnp.zeros((r, HEADS, HEAD_PAD - D_NOPE - D_ROPE), w_uq.dtype)
    w1 = jnp.concatenate([nope, rope, zr], axis=-1).reshape(r, HEADS * HEAD_PAD)
    w2 = jnp.concatenate([jnp.zeros_like(nope), rot, zr], axis=-1).reshape(r, HEADS * HEAD_PAD)
    return w1.astype(BF16), w2.astype(BF16)


def _prep_k_weights(w_uk):
    r = w_uk.shape[0]
    w = w_uk.reshape(r, HEADS, D_NOPE)
    zr = jnp.zeros((r, HEADS, HEAD_PAD - D_NOPE), w_uk.dtype)
    return jnp.concatenate([w, zr], axis=-1).reshape(r, HEADS * HEAD_PAD).astype(BF16)


def _rope_freqs():
    inv_freq = 1.0 / (ROPE_THETA ** (jnp.arange(0, D_ROPE, 2, dtype=F32) / D_ROPE))
    return inv_freq.reshape(D_ROPE // 2, 1)


def kernel(x, c, positions, w_ada, b_ada, norm1_g, w_in, q_norm_g, kv_norm_g, w_uq, w_uk, w_uv,
           conv_w, conv_b, attn_out_g, conv_out_g, w_out, norm2_g, w_router_group, b_router_group,
           w_router_expert, b_router_expert, w_exp_gate, w_exp_up, w_exp_down, final_g):
    bsz, seq, d = x.shape
    depth = w_ada.shape[0]
    q_rank = q_norm_g.shape[-1]
    kv_rank = kv_norm_g.shape[-1]
    conv_dim = conv_w.shape[-1]
    t = bsz * seq
    assert seq % max(TM_PROJ, TQ, TM_OUT, TM_COMB) == 0 and TQ % TK == 0 and TM_PROJ % TK == 0
    assert CHUNK & (CHUNK - 1) == 0 and TQ % CHUNK == 0
    assert t % max(TM_DEST, TM_DISP, TM_COMB) == 0 and (t * TOP_K) % BLK == 0
    assert w_uq.shape[-1] == HEADS * (D_NOPE + D_ROPE) and w_uv.shape[-1] == HEADS * D_V

    pos = positions.reshape(bsz, 1, seq)
    invf = _rope_freqs()
    row = lambda v: v.reshape(1, -1)
    for l in range(depth):
        mod = _adaln(c, w_ada[l], b_ada[l]).reshape(bsz, N_MOD, d)
        win = _prep_in_weights(w_in[l], q_rank, kv_rank, conv_dim)
        wq1, wq2 = _prep_q_weights(w_uq[l])
        wk = _prep_k_weights(w_uk[l])
        wvt = w_uv[l].T.astype(BF16)
        q, k, vt, conv = _inproj(x, mod, pos, row(norm1_g[l]), win, row(q_norm_g[l]),
                                 row(kv_norm_g[l]), wq1, wq2, wk, wvt, invf, conv_w[l],
                                 row(conv_b[l]), row(conv_out_g[l]))
        attn = _attention(q, k, vt)

        wr = jnp.concatenate(
            [w_router_expert[l], w_router_group[l],
             jnp.zeros((d, LANES - N_EXPERTS - N_GROUPS), F32)], axis=1).astype(BF16)
        br = jnp.concatenate(
            [b_router_expert[l], b_router_group[l],
             jnp.zeros((LANES - N_EXPERTS - N_GROUPS,), F32)]).reshape(1, LANES)
        x1, h2, route, rt, cnt = _outproj(x, attn, conv, mod, row(attn_out_g[l]),
                                      w_out[l].astype(BF16), row(norm2_g[l]), wr, br)
        route = route.reshape(t, LANES)
        counts = cnt[0, :N_EXPERTS].astype(jnp.int32)
        starts = jnp.cumsum(counts) - counts
        rpt = d // LANES
        dest = _dest(starts, rt)
        xs = _dispatch(dest, h2, rpt)
        items = _work_items(counts, t * TOP_K)
        wgu = jnp.concatenate([w_exp_gate[l], w_exp_up[l]], axis=-1).astype(BF16)
        ys = _experts(*items, xs, wgu, w_exp_down[l].astype(BF16), rpt)
        out = _combine(dest, ys, x1.reshape(t, d), route, mod, row(final_g), seq, l == depth - 1)
        x = out.reshape(bsz, seq, d)
    return x
```

```python
import functools
import math

import jax
import jax.numpy as jnp
from jax import lax
from jax.experimental import pallas as pl
from jax.experimental.pallas import tpu as pltpu

F32 = jnp.float32
BF16 = jnp.bfloat16

CHUNK = 64
HEADS = 8
D_NOPE = 64
D_ROPE = 32
D_V = 64
ROPE_THETA = 10000.0
CONV_WIDTH = 3
N_GROUPS = 4
EXPERTS_PER_GROUP = 8
N_EXPERTS = N_GROUPS * EXPERTS_PER_GROUP
TOP_K = 2
N_MOD = 6
RMS_EPS = 1e-6
NEG_INF = -1e30
LOG2_E = math.log2(math.e)

LANES = 128
SUBLANES = 8
HEAD_PAD = LANES
DEN_ROWS = 16
VMEM_LIMIT = 56 * 1024 * 1024

TM_PROJ = 512
TQ = 512
TK = 256
TM_OUT = 1024
TM_DEST = 2048
TM_DISP = 256
BLK = 512
SUB = 256
TM_COMB = 256
TN_ADA = 512
GATHER_AHEAD = 2
N_SLOTS = GATHER_AHEAD + 1
DISP_SLOTS = 3


def _rms(x, g):
    return x * lax.rsqrt(jnp.mean(x * x, axis=-1, keepdims=True) + RMS_EPS) * g


def _dot(a, b):
    return jnp.dot(a, b, preferred_element_type=F32)


def _dot_nt(a, b):
    return lax.dot_general(a, b, (((1,), (1,)), ((), ())), preferred_element_type=F32)


def _store_token_rows(ref, x):
    n, d = x.shape
    r = d // LANES
    for j in range(r):
        ref[pl.ds(j, n, stride=r), :] = x[:, j * LANES:(j + 1) * LANES]


def _load_token_rows(ref, n, r):
    return jnp.concatenate([ref[pl.ds(j, n, stride=r), :] for j in range(r)], axis=1)


def _split_bf16(x):
    hi = x.astype(BF16)
    lo = (x - hi.astype(F32)).astype(BF16)
    return hi, lo


def _adaln_kernel(c_ref, w_ref, b_ref, o_ref):
    c = c_ref[...]
    ca = c * jax.nn.sigmoid(c)
    c_hi, c_lo = _split_bf16(ca)
    w_hi, w_lo = _split_bf16(w_ref[...])
    acc = _dot(c_hi, w_hi) + _dot(c_hi, w_lo) + _dot(c_lo, w_hi)
    o_ref[...] = acc + b_ref[...]


def _adaln(c, w, b):
    bsz, d = c.shape
    n = w.shape[1]
    return pl.pallas_call(
        _adaln_kernel,
        out_shape=jax.ShapeDtypeStruct((bsz, n), F32),
        grid=(n // TN_ADA,),
        in_specs=[
            pl.BlockSpec((bsz, d), lambda j: (0, 0)),
            pl.BlockSpec((d, TN_ADA), lambda j: (0, j)),
            pl.BlockSpec((1, TN_ADA), lambda j: (0, j)),
        ],
        out_specs=pl.BlockSpec((bsz, TN_ADA), lambda j: (0, j)),
        compiler_params=pltpu.CompilerParams(
            dimension_semantics=("arbitrary",), vmem_limit_bytes=VMEM_LIMIT),
        name="adaln",
    )(c, w, b.reshape(1, n))


def _inproj_kernel(x_ref, mod_ref, pos_ref, g1_ref, win_ref, gq_ref, gkv_ref, wq1_ref, wq2_ref,
                   wk_ref, wvt_ref, invf_ref, cw_ref, cb_ref, gc_ref,
                   q_out, k_out, vt_out, conv_out, zbuf, *, tm, q_rank, kv_rank, conv_dim):
    si = pl.program_id(1)
    x = x_ref[0]
    mod = mod_ref[0]
    h = (_rms(x, g1_ref[...]) * (1.0 + mod[1:2]) + mod[0:1]).astype(BF16)
    proj = _dot(h, win_ref[...])

    o_kv = q_rank
    o_kr = o_kv + kv_rank
    o_u = o_kr + 2 * LANES
    q_c = proj[:, 0:q_rank]
    kv_c = proj[:, o_kv:o_kv + kv_rank]
    kr = proj[:, o_kr:o_kr + LANES]
    kr_rot = proj[:, o_kr + LANES:o_kr + 2 * LANES]
    u = proj[:, o_u:o_u + conv_dim]
    b_gate = proj[:, o_u + conv_dim:o_u + 2 * conv_dim]
    c_gate = proj[:, o_u + 2 * conv_dim:o_u + 3 * conv_dim]

    ang_t = invf_ref[...] * pos_ref[0].astype(F32)

    def head_lanes(tab_t):
        zl = jnp.zeros((D_NOPE, tm), F32)
        zr = jnp.zeros((LANES - D_NOPE - D_ROPE, tm), F32)
        return jnp.concatenate([zl, tab_t, tab_t, zr], axis=0).T

    cos_t = head_lanes(jnp.cos(ang_t))
    sin_t = head_lanes(jnp.sin(ang_t))
    lane = lax.broadcasted_iota(jnp.int32, (1, LANES), 1)
    q_scale = LOG2_E / math.sqrt(D_NOPE + D_ROPE)
    q_cos = jnp.where(lane < D_NOPE, q_scale, cos_t * q_scale)
    q_sin = sin_t * q_scale

    qn = _rms(q_c, gq_ref[...]).astype(BF16)
    qa = _dot(qn, wq1_ref[...])
    qb = _dot(qn, wq2_ref[...])
    kvn = _rms(kv_c, gkv_ref[...]).astype(BF16)
    kn = _dot(kvn, wk_ref[...])
    k_rope = kr * cos_t + kr_rot * sin_t
    for hd in range(HEADS):
        sl = slice(hd * HEAD_PAD, (hd + 1) * HEAD_PAD)
        q_out[0, hd] = (qa[:, sl] * q_cos + qb[:, sl] * q_sin).astype(BF16)
        k_out[0, hd] = (kn[:, sl] + k_rope).astype(BF16)

    vt = _dot_nt(wvt_ref[...], kvn)
    for j in range(tm // TK):
        vt_out[0, j] = vt[:, j * TK:(j + 1) * TK].astype(BF16)

    z = c_gate * u

    @pl.when(si == 0)
    def _():
        zbuf[0:SUBLANES, :] = jnp.zeros((SUBLANES, conv_dim), F32)

    zbuf[SUBLANES:SUBLANES + tm, :] = z
    z1 = zbuf[SUBLANES - 1:SUBLANES - 1 + tm, :]
    z2 = zbuf[SUBLANES - 2:SUBLANES - 2 + tm, :]
    cw = cw_ref[...]
    zc = cw[0:1] * z2 + cw[1:2] * z1 + cw[2:3] * z + cb_ref[...]
    conv = b_gate * zc
    conv_out[0] = _rms(conv, gc_ref[...]).astype(BF16)
    zbuf[0:SUBLANES, :] = zbuf[tm:tm + SUBLANES, :]


def _inproj(x, mod, pos, g1, win, gq, gkv, wq1, wq2, wk, wvt, invf, cw, cb, gc):
    bsz, seq, d = x.shape
    tm = TM_PROJ
    q_rank, kv_rank = wq1.shape[0], wk.shape[0]
    conv_dim = cw.shape[1]
    hv = wvt.shape[0]
    n_s = seq // tm
    const2 = lambda b, s: (0, 0)
    kern = functools.partial(_inproj_kernel, tm=tm, q_rank=q_rank, kv_rank=kv_rank,
                             conv_dim=conv_dim)
    return pl.pallas_call(
        kern,
        out_shape=(
            jax.ShapeDtypeStruct((bsz, HEADS, seq, HEAD_PAD), BF16),
            jax.ShapeDtypeStruct((bsz, HEADS, seq, HEAD_PAD), BF16),
            jax.ShapeDtypeStruct((bsz, seq // TK, hv, TK), BF16),
            jax.ShapeDtypeStruct((bsz, seq, conv_dim), BF16),
        ),
        grid=(bsz, n_s),
        in_specs=[
            pl.BlockSpec((1, tm, d), lambda b, s: (b, s, 0)),
            pl.BlockSpec((1, N_MOD, d), lambda b, s: (b, 0, 0)),
            pl.BlockSpec((1, 1, tm), lambda b, s: (b, 0, s)),
            pl.BlockSpec((1, d), const2),
            pl.BlockSpec(win.shape, const2),
            pl.BlockSpec((1, q_rank), const2),
            pl.BlockSpec((1, kv_rank), const2),
            pl.BlockSpec(wq1.shape, const2),
            pl.BlockSpec(wq2.shape, const2),
            pl.BlockSpec(wk.shape, const2),
            pl.BlockSpec(wvt.shape, const2),
            pl.BlockSpec(invf.shape, const2),
            pl.BlockSpec(cw.shape, const2),
            pl.BlockSpec((1, conv_dim), const2),
            pl.BlockSpec((1, conv_dim), const2),
        ],
        out_specs=(
            pl.BlockSpec((1, HEADS, tm, HEAD_PAD), lambda b, s: (b, 0, s, 0)),
            pl.BlockSpec((1, HEADS, tm, HEAD_PAD), lambda b, s: (b, 0, s, 0)),
            pl.BlockSpec((1, tm // TK, hv, TK), lambda b, s: (b, s, 0, 0)),
            pl.BlockSpec((1, tm, conv_dim), lambda b, s: (b, s, 0)),
        ),
        scratch_shapes=[pltpu.VMEM((tm + 2 * SUBLANES, conv_dim), F32)],
        compiler_params=pltpu.CompilerParams(
            dimension_semantics=("arbitrary", "arbitrary"), vmem_limit_bytes=VMEM_LIMIT),
        name="inproj",
    )(x, mod, pos, g1, win, gq, gkv, wq1, wq2, wk, wvt, invf, cw, cb, gc)


def _attn_kernel(q_ref, k_ref, vt_ref, o_ref, *, n_q):
    row = lax.broadcasted_iota(jnp.int32, (TQ, TQ), 0)
    col = lax.broadcasted_iota(jnp.int32, (TQ, TQ), 1)
    shift = CHUNK.bit_length() - 1
    diag_ok = (row >> shift) <= (col >> shift)
    vt_per_step = TQ // TK

    ones_rows = jnp.ones((DEN_ROWS, TQ), BF16)

    def scores(hh, qi, j):
        return _dot_nt(k_ref[0, hh, j * TQ:(j + 1) * TQ, :], q_ref[0, hh, qi * TQ:(qi + 1) * TQ, :])

    def softmax_pv(hh, j, s_t, state, masked):
        m, acc = state
        if masked:
            s_t = jnp.where(diag_ok, s_t, NEG_INF)
        m_new = jnp.maximum(m, jnp.max(s_t, axis=0, keepdims=True))
        p = jnp.exp2(s_t - m_new).astype(BF16)
        alpha = jnp.exp2(m - m_new)
        v_t = jnp.concatenate(
            [vt_ref[0, j * vt_per_step + i, hh * D_V:(hh + 1) * D_V, :]
             for i in range(vt_per_step)], axis=1)
        v_aug = jnp.concatenate([v_t, ones_rows], axis=0)
        acc_new = alpha * acc + _dot(v_aug, p)
        return m_new, acc_new

    steps = [(qi, j) for qi in range(n_q) for j in range(qi + 1)]
    init = (jnp.full((1, TQ), NEG_INF, F32), jnp.zeros((D_V + DEN_ROWS, TQ), F32))
    s_cur = tuple(scores(hh, *steps[0]) for hh in range(2))
    states = None
    for idx, (qi, j) in enumerate(steps):
        s_next = None
        if idx + 1 < len(steps):
            s_next = tuple(scores(hh, *steps[idx + 1]) for hh in range(2))
        if j == 0:
            states = (init, init)
        states = tuple(softmax_pv(hh, j, s_cur[hh], states[hh], j == qi) for hh in range(2))
        if j == qi:
            o_t = jnp.concatenate([acc[:D_V] / acc[D_V:D_V + 1] for (_, acc) in states],
                                  axis=0)
            o_ref[0, qi * TQ:(qi + 1) * TQ, :] = o_t.T.astype(BF16)
        s_cur = s_next


def _attention(q, k, vt):
    bsz, _, seq, _ = q.shape
    n_pair = HEADS // 2
    return pl.pallas_call(
        functools.partial(_attn_kernel, n_q=seq // TQ),
        out_shape=jax.ShapeDtypeStruct((bsz, seq, HEADS * D_V), BF16),
        grid=(bsz, n_pair),
        in_specs=[
            pl.BlockSpec((1, 2, seq, HEAD_PAD), lambda b, p: (b, p, 0, 0)),
            pl.BlockSpec((1, 2, seq, HEAD_PAD), lambda b, p: (b, p, 0, 0)),
            pl.BlockSpec((1, seq // TK, 2 * D_V, TK), lambda b, p: (b, 0, p, 0)),
        ],
        out_specs=pl.BlockSpec((1, seq, 2 * D_V), lambda b, p: (b, 0, p)),
        compiler_params=pltpu.CompilerParams(
            dimension_semantics=("arbitrary", "arbitrary"), vmem_limit_bytes=VMEM_LIMIT),
        name="attn",
    )(q, k, vt)


def _outproj_kernel(x_ref, attn_ref, conv_ref, mod_ref, ga_ref, wout_ref, g2_ref, wr_ref, br_ref,
                    tri_ref, x1_out, h2_out, route_out, rt_out, cnt_out, base_ref):
    first = (pl.program_id(0) == 0) & (pl.program_id(1) == 0)

    @pl.when(first)
    def _():
        base_ref[...] = jnp.zeros(base_ref.shape, F32)

    mod = mod_ref[0]
    an = _rms(attn_ref[0].astype(F32), ga_ref[...]).astype(BF16)
    mixed = jnp.concatenate([an, conv_ref[0]], axis=-1)
    y = _dot(mixed, wout_ref[...])
    x1 = x_ref[0] + mod[2:3] * y
    x1_out[0] = x1
    h2 = _rms(x1, g2_ref[...]) * (1.0 + mod[4:5]) + mod[3:4]
    _store_token_rows(h2_out, h2)
    logits = _dot(h2.astype(BF16), wr_ref[...]) + br_ref[...]

    tm = logits.shape[0]
    lane = lax.broadcasted_iota(jnp.int32, (tm, LANES), 1).astype(F32)
    big = float(LANES)
    is_grp = (lane >= N_EXPERTS) & (lane < N_EXPERTS + N_GROUPS)
    gl = jnp.where(is_grp, logits, -jnp.inf)
    gmax = jnp.max(gl, axis=-1, keepdims=True)
    gidx = jnp.min(jnp.where(gl == gmax, lane, big), axis=-1, keepdims=True) - N_EXPERTS
    p_group = 1.0 / jnp.sum(jnp.where(is_grp, jnp.exp(logits - gmax), 0.0), axis=-1, keepdims=True)
    lo = gidx * EXPERTS_PER_GROUP
    in_grp = (lane >= lo) & (lane < lo + EXPERTS_PER_GROUP)
    el = jnp.where(in_grp, logits, -jnp.inf)
    v1 = jnp.max(el, axis=-1, keepdims=True)
    i1 = jnp.min(jnp.where(el == v1, lane, big), axis=-1, keepdims=True)
    el2 = jnp.where(lane == i1, -jnp.inf, el)
    v2 = jnp.max(el2, axis=-1, keepdims=True)
    i2 = jnp.min(jnp.where(el2 == v2, lane, big), axis=-1, keepdims=True)
    t = jnp.exp(v2 - v1)
    w1 = p_group / (1.0 + t)
    w2 = p_group * t / (1.0 + t)

    oh1 = lane == i1
    oh2 = lane == i2
    cnt = jnp.where(oh1 | oh2, 1.0, 0.0)
    prefix = _dot(tri_ref[...], cnt.astype(BF16)) + base_ref[0:1, :]
    r1 = jnp.sum(jnp.where(oh1, prefix, 0.0), axis=-1, keepdims=True)
    r2 = jnp.sum(jnp.where(oh2, prefix, 0.0), axis=-1, keepdims=True)
    new_base = base_ref[...] + jnp.sum(cnt, axis=0, keepdims=True)
    base_ref[...] = new_base
    cnt_out[...] = new_base

    route = jnp.where(lane == 0.0, i1, jnp.where(lane == 1.0, i2, jnp.where(
        lane == 2.0, r1, jnp.where(lane == 3.0, r2, jnp.where(
            lane == 4.0, w1, jnp.where(lane == 5.0, w2, 0.0))))))
    route_out[0] = route
    rt_out[...] = route.T[0:SUBLANES, :].astype(jnp.int32)


def _outproj(x, attn, conv, mod, ga, wout, g2, wr, br):
    bsz, seq, d = x.shape
    tm = TM_OUT
    aw = attn.shape[-1]
    cwd = conv.shape[-1]
    n_s = seq // tm
    rpt = d // LANES
    tri = (lax.broadcasted_iota(jnp.int32, (tm, tm), 1)
           < lax.broadcasted_iota(jnp.int32, (tm, tm), 0)).astype(BF16)
    const2 = lambda b, s: (0, 0)
    tile3 = lambda b, s: (b, s, 0)
    return pl.pallas_call(
        _outproj_kernel,
        out_shape=(
            jax.ShapeDtypeStruct((bsz, seq, d), F32),
            jax.ShapeDtypeStruct((bsz * seq * rpt, LANES), F32),
            jax.ShapeDtypeStruct((bsz, seq, LANES), F32),
            jax.ShapeDtypeStruct((SUBLANES, bsz * seq), jnp.int32),
            jax.ShapeDtypeStruct((SUBLANES, LANES), F32),
        ),
        grid=(bsz, n_s),
        in_specs=[
            pl.BlockSpec((1, tm, d), tile3),
            pl.BlockSpec((1, tm, aw), tile3),
            pl.BlockSpec((1, tm, cwd), tile3),
            pl.BlockSpec((1, N_MOD, d), lambda b, s: (b, 0, 0)),
            pl.BlockSpec((1, aw), const2),
            pl.BlockSpec(wout.shape, const2),
            pl.BlockSpec((1, d), const2),
            pl.BlockSpec(wr.shape, const2),
            pl.BlockSpec((1, LANES), const2),
            pl.BlockSpec((tm, tm), const2),
        ],
        out_specs=(
            pl.BlockSpec((1, tm, d), tile3),
            pl.BlockSpec((tm * rpt, LANES), lambda b, s: (b * n_s + s, 0)),
            pl.BlockSpec((1, tm, LANES), tile3),
            pl.BlockSpec((SUBLANES, tm), lambda b, s: (0, b * n_s + s)),
            pl.BlockSpec((SUBLANES, LANES), const2),
        ),
        scratch_shapes=[pltpu.VMEM((SUBLANES, LANES), F32)],
        compiler_params=pltpu.CompilerParams(
            dimension_semantics=("arbitrary", "arbitrary"), vmem_limit_bytes=VMEM_LIMIT),
        name="outproj",
    )(x, attn, conv, mod, ga, wout, g2, wr, br, tri)


def _dest_kernel(starts_ref, rt_ref, dest_out):
    rt = rt_ref[...]
    ex = rt[0:TOP_K, :]
    start = jnp.zeros(ex.shape, jnp.int32)
    for e in range(N_EXPERTS):
        start = jnp.where(ex == e, starts_ref[e], start)
    dest = start + rt[TOP_K:2 * TOP_K, :]
    dest_out[...] = jnp.concatenate(
        [dest, jnp.zeros((SUBLANES - TOP_K, dest.shape[1]), jnp.int32)], axis=0)


def _dest(starts, rt):
    t = rt.shape[1]
    tm = TM_DEST
    grid_spec = pltpu.PrefetchScalarGridSpec(
        num_scalar_prefetch=1,
        grid=(t // tm,),
        in_specs=[pl.BlockSpec((SUBLANES, tm), lambda i, st: (0, i))],
        out_specs=pl.BlockSpec((SUBLANES, tm), lambda i, st: (0, i)),
    )
    return pl.pallas_call(
        _dest_kernel,
        out_shape=jax.ShapeDtypeStruct((SUBLANES, t), jnp.int32),
        grid_spec=grid_spec,
        compiler_params=pltpu.CompilerParams(
            dimension_semantics=("arbitrary",), vmem_limit_bytes=VMEM_LIMIT),
        name="dest",
    )(starts, rt)


def _token_copy(src, src_tok, dst, dst_tok, sem, rpt):
    return pltpu.make_async_copy(src.at[pl.ds(src_tok * rpt, rpt)], dst.at[pl.ds(dst_tok * rpt, rpt)],
                                 sem)


def _dispatch_kernel(dest_ref, h_hbm, xs_hbm, hbuf, sem_in, sem_out, *, tm, rpt):
    i = pl.program_id(0)
    n = pl.num_programs(0)
    slot = i % DISP_SLOTS

    def tile_load(tile, slot_idx):
        return pltpu.make_async_copy(h_hbm.at[pl.ds(tile * tm * rpt, tm * rpt)], hbuf.at[slot_idx],
                                     sem_in.at[slot_idx])

    def drain(slot_idx):
        def body(t, carry):
            for k in range(TOP_K):
                _token_copy(hbuf.at[slot_idx], 0, xs_hbm, 0, sem_out.at[slot_idx], rpt).wait()
            return carry
        lax.fori_loop(0, tm, body, 0, unroll=8)

    @pl.when(i == 0)
    def _():
        tile_load(0, 0).start()

    @pl.when((i == 0) & (n > 1))
    def _():
        tile_load(1, 1).start()

    tile_load(i, slot).wait()
    for t in range(tm):
        for k in range(TOP_K):
            _token_copy(hbuf.at[slot], t, xs_hbm, dest_ref[k, t], sem_out.at[slot], rpt).start(
                priority=k)

    refill = (i + 2) % DISP_SLOTS

    @pl.when(i > 0)
    def _():
        drain(refill)

    @pl.when(i + 2 < n)
    def _():
        tile_load(i + 2, refill).start()

    @pl.when(i == n - 1)
    def _():
        drain(slot)


def _dispatch(dest, h, rpt):
    t = h.shape[0] // rpt
    tm = TM_DISP
    return pl.pallas_call(
        functools.partial(_dispatch_kernel, tm=tm, rpt=rpt),
        out_shape=jax.ShapeDtypeStruct((t * TOP_K * rpt, LANES), h.dtype),
        grid=(t // tm,),
        in_specs=[
            pl.BlockSpec((SUBLANES, tm), lambda i: (0, i), memory_space=pltpu.SMEM),
            pl.BlockSpec(memory_space=pl.ANY),
        ],
        out_specs=pl.BlockSpec(memory_space=pl.ANY),
        scratch_shapes=[pltpu.VMEM((DISP_SLOTS, tm * rpt, LANES), h.dtype),
                        pltpu.SemaphoreType.DMA((DISP_SLOTS,)),
                        pltpu.SemaphoreType.DMA((DISP_SLOTS,))],
        compiler_params=pltpu.CompilerParams(
            dimension_semantics=("arbitrary",), vmem_limit_bytes=VMEM_LIMIT),
        name="dispatch",
    )(dest, h)


def _experts_kernel(blk_ref, exp_ref, lo_ref, hi_ref, x_ref, wg_ref, wu_ref, wd_ref, o_ref, acc_ref,
                    wgu_bf, wd_bf, *, n_items, rpt):
    i = pl.program_id(0)
    lo = lo_ref[i]
    hi = hi_ref[i]
    blk = blk_ref[i]
    first = (i == 0) | (blk != blk_ref[jnp.maximum(i - 1, 0)])

    @pl.when(i == 0)
    def _():
        acc_ref[...] = jnp.zeros(acc_ref.shape, F32)

    @pl.when((i == 0) | (exp_ref[i] != exp_ref[jnp.maximum(i - 1, 0)]))
    def _():
        de = wd_ref.shape[1]
        wgu_bf[:, :de] = wg_ref[0].astype(BF16)
        wgu_bf[:, de:] = wu_ref[0].astype(BF16)
        wd_bf[...] = wd_ref[0].astype(BF16)

    @pl.when(hi > lo)
    def _():
        for sb in range(BLK // SUB):
            rows = pl.ds(sb * SUB, SUB)
            tok_rows = pl.ds(sb * SUB * rpt, SUB * rpt)
            x = _load_token_rows(x_ref.at[tok_rows], SUB, rpt).astype(BF16)
            gu = _dot(x, wgu_bf[...])
            de = gu.shape[1] // 2
            g = gu[:, :de]
            a = (g * jax.nn.sigmoid(g) * gu[:, de:]).astype(BF16)
            y = _dot(a, wd_bf[...])
            row = lax.broadcasted_iota(jnp.int32, (SUB, 1), 0) + sb * SUB
            keep = (row >= lo) & (row < hi)
            acc = jnp.where(first, 0.0, acc_ref[rows, :]) + jnp.where(keep, y, 0.0)
            acc_ref[rows, :] = acc
            _store_token_rows(o_ref.at[tok_rows], acc)


def _experts(item_blk, item_exp, item_lo, item_hi, xs, wg, wu, wd, rpt):
    a = xs.shape[0] // rpt
    d = rpt * LANES
    de = wd.shape[1]
    n_items = item_blk.shape[0]
    grid_spec = pltpu.PrefetchScalarGridSpec(
        num_scalar_prefetch=4,
        grid=(n_items,),
        in_specs=[
            pl.BlockSpec((BLK * rpt, LANES), lambda i, blk, ex, lo, hi: (blk[i], 0)),
            pl.BlockSpec((1, d, de), lambda i, blk, ex, lo, hi: (ex[i], 0, 0)),
            pl.BlockSpec((1, d, de), lambda i, blk, ex, lo, hi: (ex[i], 0, 0)),
            pl.BlockSpec((1, de, d), lambda i, blk, ex, lo, hi: (ex[i], 0, 0)),
        ],
        out_specs=pl.BlockSpec((BLK * rpt, LANES), lambda i, blk, ex, lo, hi: (blk[i], 0)),
        scratch_shapes=[pltpu.VMEM((BLK, d), F32), pltpu.VMEM((d, 2 * de), BF16),
                        pltpu.VMEM((de, d), BF16)],
    )
    return pl.pallas_call(
        functools.partial(_experts_kernel, n_items=n_items, rpt=rpt),
        out_shape=jax.ShapeDtypeStruct((a * rpt, LANES), F32),
        grid_spec=grid_spec,
        compiler_params=pltpu.CompilerParams(
            dimension_semantics=("arbitrary",), vmem_limit_bytes=VMEM_LIMIT),
        name="experts",
    )(item_blk, item_exp, item_lo, item_hi, xs, wg, wu, wd)


def _work_items(counts, n_rows):
    n_blocks = n_rows // BLK
    n_items = n_blocks + N_EXPERTS - 1
    counts = counts.astype(jnp.int32)
    ends = jnp.cumsum(counts)
    starts = ends - counts
    first_blk = starts // BLK
    last_blk = jnp.maximum(ends - 1, 0) // BLK
    n_e = jnp.where(counts > 0, last_blk - first_blk + 1, 0)
    item_end = jnp.cumsum(n_e)
    item_start = item_end - n_e
    idx = jnp.arange(n_items, dtype=jnp.int32)
    total = item_end[-1]
    ex = jnp.minimum(jnp.sum((idx[:, None] >= item_end[None, :]).astype(jnp.int32), axis=1),
                     N_EXPERTS - 1)
    onehot = (ex[:, None] == jnp.arange(N_EXPERTS, dtype=jnp.int32)[None, :]).astype(jnp.int32)
    pick = lambda table: jnp.sum(onehot * table[None, :], axis=1)
    blk = pick(first_blk) + (idx - pick(item_start))
    lo = jnp.maximum(pick(starts), blk * BLK) - blk * BLK
    hi = jnp.minimum(pick(ends), (blk + 1) * BLK) - blk * BLK
    valid = idx < total
    last_ex = jnp.max(jnp.where(valid, ex, 0))
    blk = jnp.where(valid, blk, n_blocks - 1).astype(jnp.int32)
    ex = jnp.where(valid, ex, last_ex).astype(jnp.int32)
    lo = jnp.where(valid, lo, 0).astype(jnp.int32)
    hi = jnp.where(valid, hi, 0).astype(jnp.int32)
    return blk, ex, lo, hi


def _combine_kernel(dest_ref, dest_one_ref, dest_ahead_ref, ys_hbm, x1_ref, route_ref, mod_ref,
                    gf_ref, o_ref, ybuf, sems, *, tm, final_norm, rpt):
    i = pl.program_id(0)
    slot = i % N_SLOTS

    def gather(d_ref, slot_idx):
        for t in range(tm):
            for k in range(TOP_K):
                _token_copy(ys_hbm, d_ref[k, t], ybuf.at[slot_idx, k], t, sems.at[slot_idx],
                            rpt).start(priority=k)

    def drain(slot_idx):
        def body(t, carry):
            for k in range(TOP_K):
                _token_copy(ys_hbm, 0, ybuf.at[slot_idx, k], 0, sems.at[slot_idx], rpt).wait()
            return carry
        lax.fori_loop(0, tm, body, 0, unroll=8)

    @pl.when(i == 0)
    def _():
        gather(dest_ref, 0)
        gather(dest_one_ref, 1)

    drain(slot)
    r = route_ref[...]
    y = (r[:, 4:5] * _load_token_rows(ybuf.at[slot, 0], tm, rpt)
         + r[:, 5:6] * _load_token_rows(ybuf.at[slot, 1], tm, rpt))
    mod = mod_ref[0]
    x2 = x1_ref[...] + mod[5:6] * y
    o_ref[...] = _rms(x2, gf_ref[...]) if final_norm else x2

    gather(dest_ahead_ref, (i + GATHER_AHEAD) % N_SLOTS)

    @pl.when(i == pl.num_programs(0) - 1)
    def _():
        for extra in range(1, N_SLOTS):
            drain((i + extra) % N_SLOTS)


def _combine(dest, ys, x1, route, mod, gf, seq, final_norm):
    t, d = x1.shape
    tm = TM_COMB
    n_steps = t // tm
    per_b = seq // tm
    rpt = d // LANES
    return pl.pallas_call(
        functools.partial(_combine_kernel, tm=tm, final_norm=final_norm, rpt=rpt),
        out_shape=jax.ShapeDtypeStruct((t, d), F32),
        grid=(n_steps,),
        in_specs=[
            pl.BlockSpec((SUBLANES, tm), lambda i: (0, i), memory_space=pltpu.SMEM),
            pl.BlockSpec((SUBLANES, tm), lambda i: (0, min(1, n_steps - 1)),
                         memory_space=pltpu.SMEM),
            pl.BlockSpec((SUBLANES, tm), lambda i: (0, jnp.minimum(i + GATHER_AHEAD, n_steps - 1)),
                         memory_space=pltpu.SMEM),
            pl.BlockSpec(memory_space=pl.ANY),
            pl.BlockSpec((tm, d), lambda i: (i, 0)),
            pl.BlockSpec((tm, LANES), lambda i: (i, 0)),
            pl.BlockSpec((1, N_MOD, d), lambda i: (i // per_b, 0, 0)),
            pl.BlockSpec((1, d), lambda i: (0, 0)),
        ],
        out_specs=pl.BlockSpec((tm, d), lambda i: (i, 0)),
        scratch_shapes=[pltpu.VMEM((N_SLOTS, TOP_K, tm * rpt, LANES), F32),
                        pltpu.SemaphoreType.DMA((N_SLOTS,))],
        compiler_params=pltpu.CompilerParams(
            dimension_semantics=("arbitrary",), vmem_limit_bytes=VMEM_LIMIT),
        name="combine",
    )(dest, dest, dest, ys, x1, route, mod, gf)


def _prep_in_weights(w_in, q_rank, kv_rank, conv_dim):
    o1 = q_rank
    o2 = o1 + kv_rank
    o3 = o2 + D_ROPE
    d = w_in.shape[0]
    w_kr = w_in[:, o2:o3]
    half = D_ROPE // 2
    w_kr_rot = jnp.concatenate([-w_kr[:, half:], w_kr[:, :half]], axis=1)
    zl = jnp.zeros((d, D_NOPE), w_in.dtype)
    zr = jnp.zeros((d, LANES - D_NOPE - D_ROPE), w_in.dtype)
    return jnp.concatenate(
        [w_in[:, :o2], zl, w_kr, zr, zl, w_kr_rot, zr, w_in[:, o3:]], axis=1).astype(BF16)


def _prep_q_weights(w_uq):
    r = w_uq.shape[0]
    w = w_uq.reshape(r, HEADS, D_NOPE + D_ROPE)
    nope, rope = w[..., :D_NOPE], w[..., D_NOPE:]
    half = D_ROPE // 2
    rot = jnp.concatenate([-rope[..., half:], rope[..., :half]], axis=-1)
    zr = jnp.zeros((r, HEADS, HEAD_PAD - D_NOPE - D_ROPE), w_uq.dtype)
    w1 = jnp.concatenate([nope, rope, zr], axis=-1).reshape(r, HEADS * HEAD_PAD)
    w2 = jnp.concatenate([jnp.zeros_like(nope), rot, zr], axis=-1).reshape(r, HEADS * HEAD_PAD)
    return w1.astype(BF16), w2.astype(BF16)


def _prep_k_weights(w_uk):
    r = w_uk.shape[0]
    w = w_uk.reshape(r, HEADS, D_NOPE)
    zr = jnp.zeros((r, HEADS, HEAD_PAD - D_NOPE), w_uk.dtype)
    return jnp.concatenate([w, zr], axis=-1).reshape(r, HEADS * HEAD_PAD).astype(BF16)


def _rope_freqs():
    inv_freq = 1.0 / (ROPE_THETA ** (jnp.arange(0, D_ROPE, 2, dtype=F32) / D_ROPE))
    return inv_freq.reshape(D_ROPE // 2, 1)


def kernel(x, c, positions, w_ada, b_ada, norm1_g, w_in, q_norm_g, kv_norm_g, w_uq, w_uk, w_uv,
           conv_w, conv_b, attn_out_g, conv_out_g, w_out, norm2_g, w_router_group, b_router_group,
           w_router_expert, b_router_expert, w_exp_gate, w_exp_up, w_exp_down, final_g):
    bsz, seq, d = x.shape
    depth = w_ada.shape[0]
    q_rank = q_norm_g.shape[-1]
    kv_rank = kv_norm_g.shape[-1]
    conv_dim = conv_w.shape[-1]
    t = bsz * seq
    assert seq % max(TM_PROJ, TQ, TM_OUT, TM_COMB) == 0 and TQ % TK == 0 and TM_PROJ % TK == 0
    assert CHUNK & (CHUNK - 1) == 0 and TQ % CHUNK == 0
    assert t % max(TM_DEST, TM_DISP, TM_COMB) == 0 and (t * TOP_K) % BLK == 0
    assert w_uq.shape[-1] == HEADS * (D_NOPE + D_ROPE) and w_uv.shape[-1] == HEADS * D_V

    pos = positions.reshape(bsz, 1, seq)
    invf = _rope_freqs()
    row = lambda v: v.reshape(1, -1)
    for l in range(depth):
        mod = _adaln(c, w_ada[l], b_ada[l]).reshape(bsz, N_MOD, d)
        win = _prep_in_weights(w_in[l], q_rank, kv_rank, conv_dim)
        wq1, wq2 = _prep_q_weights(w_uq[l])
        wk = _prep_k_weights(w_uk[l])
        wvt = w_uv[l].T.astype(BF16)
        q, k, vt, conv = _inproj(x, mod, pos, row(norm1_g[l]), win, row(q_norm_g[l]),
                                 row(kv_norm_g[l]), wq1, wq2, wk, wvt, invf, conv_w[l],
                                 row(conv_b[l]), row(conv_out_g[l]))
        attn = _attention(q, k, vt)

        wr = jnp.concatenate(
            [w_router_expert[l], w_router_group[l],
             jnp.zeros((d, LANES - N_EXPERTS - N_GROUPS), F32)], axis=1).astype(BF16)
        br = jnp.concatenate(
            [b_router_expert[l], b_router_group[l],
             jnp.zeros((LANES - N_EXPERTS - N_GROUPS,), F32)]).reshape(1, LANES)
        x1, h2, route, rt, cnt = _outproj(x, attn, conv, mod, row(attn_out_g[l]),
                                      w_out[l].astype(BF16), row(norm2_g[l]), wr, br)
        route = route.reshape(t, LANES)
        counts = cnt[0, :N_EXPERTS].astype(jnp.int32)
        starts = jnp.cumsum(counts) - counts
        rpt = d // LANES
        dest = _dest(starts, rt)
        xs = _dispatch(dest, h2, rpt)
        items = _work_items(counts, t * TOP_K)
        ys = _experts(*items, xs, w_exp_gate[l], w_exp_up[l], w_exp_down[l], rpt)
        out = _combine(dest, ys, x1.reshape(t, d), route, mod, row(final_g), seq, l == depth - 1)
        x = out.reshape(bsz, seq, d)
    return x
```

```python
import functools
import math

import jax
import jax.numpy as jnp
from jax import lax
from jax.experimental import pallas as pl
from jax.experimental.pallas import tpu as pltpu

F32 = jnp.float32
BF16 = jnp.bfloat16

CHUNK = 64
HEADS = 8
D_NOPE = 64
D_ROPE = 32
D_V = 64
ROPE_THETA = 10000.0
CONV_WIDTH = 3
N_GROUPS = 4
EXPERTS_PER_GROUP = 8
N_EXPERTS = N_GROUPS * EXPERTS_PER_GROUP
TOP_K = 2
N_MOD = 6
RMS_EPS = 1e-6
NEG_INF = -1e30
LOG2_E = math.log2(math.e)

LANES = 128
SUBLANES = 8
HEAD_PAD = LANES
DEN_ROWS = 16
VMEM_LIMIT = 56 * 1024 * 1024

TM_PROJ = 512
TQ = 512
TK = 256
TM_OUT = 1024
TM_DEST = 2048
TM_DISP = 256
BLK = 512
SUB = 256
TM_COMB = 256
TN_ADA = 512
TRI = 256
GATHER_AHEAD = 2
N_SLOTS = GATHER_AHEAD + 1
DISP_SLOTS = 3


def _rms(x, g):
    return x * lax.rsqrt(jnp.mean(x * x, axis=-1, keepdims=True) + RMS_EPS) * g


def _dot(a, b):
    return jnp.dot(a, b, preferred_element_type=F32)


def _dot_nt(a, b):
    return lax.dot_general(a, b, (((1,), (1,)), ((), ())), preferred_element_type=F32)


def _store_token_rows(ref, x):
    n, d = x.shape
    r = d // LANES
    for j in range(r):
        ref[pl.ds(j, n, stride=r), :] = x[:, j * LANES:(j + 1) * LANES]


def _load_token_rows(ref, n, r):
    return jnp.concatenate([ref[pl.ds(j, n, stride=r), :] for j in range(r)], axis=1)


def _split_bf16(x):
    hi = x.astype(BF16)
    lo = (x - hi.astype(F32)).astype(BF16)
    return hi, lo


def _adaln_kernel(c_ref, w_ref, b_ref, o_ref):
    c = c_ref[...]
    ca = c * jax.nn.sigmoid(c)
    c_hi, c_lo = _split_bf16(ca)
    w_hi, w_lo = _split_bf16(w_ref[...])
    acc = _dot(c_hi, w_hi) + _dot(c_hi, w_lo) + _dot(c_lo, w_hi)
    o_ref[...] = acc + b_ref[...]


def _adaln(c, w, b):
    bsz, d = c.shape
    n = w.shape[1]
    return pl.pallas_call(
        _adaln_kernel,
        out_shape=jax.ShapeDtypeStruct((bsz, n), F32),
        grid=(n // TN_ADA,),
        in_specs=[
            pl.BlockSpec((bsz, d), lambda j: (0, 0)),
            pl.BlockSpec((d, TN_ADA), lambda j: (0, j)),
            pl.BlockSpec((1, TN_ADA), lambda j: (0, j)),
        ],
        out_specs=pl.BlockSpec((bsz, TN_ADA), lambda j: (0, j)),
        compiler_params=pltpu.CompilerParams(
            dimension_semantics=("arbitrary",), vmem_limit_bytes=VMEM_LIMIT),
        name="adaln",
    )(c, w, b.reshape(1, n))


def _inproj_kernel(x_ref, mod_ref, pos_ref, g1_ref, win_ref, gq_ref, gkv_ref, wq1_ref, wq2_ref,
                   wk_ref, wvt_ref, invf_ref, cw_ref, cb_ref, gc_ref,
                   q_out, k_out, vt_out, conv_out, zbuf, *, tm, q_rank, kv_rank, conv_dim):
    si = pl.program_id(1)
    x = x_ref[0]
    mod = mod_ref[0]
    h = (_rms(x, g1_ref[...]) * (1.0 + mod[1:2]) + mod[0:1]).astype(BF16)
    proj = _dot(h, win_ref[...])

    o_kv = q_rank
    o_kr = o_kv + kv_rank
    o_u = o_kr + 2 * LANES
    q_c = proj[:, 0:q_rank]
    kv_c = proj[:, o_kv:o_kv + kv_rank]
    kr = proj[:, o_kr:o_kr + LANES]
    kr_rot = proj[:, o_kr + LANES:o_kr + 2 * LANES]
    u = proj[:, o_u:o_u + conv_dim]
    b_gate = proj[:, o_u + conv_dim:o_u + 2 * conv_dim]
    c_gate = proj[:, o_u + 2 * conv_dim:o_u + 3 * conv_dim]

    ang_t = invf_ref[...] * pos_ref[0].astype(F32)

    def head_lanes(tab_t):
        zl = jnp.zeros((D_NOPE, tm), F32)
        zr = jnp.zeros((LANES - D_NOPE - D_ROPE, tm), F32)
        return jnp.concatenate([zl, tab_t, tab_t, zr], axis=0).T

    cos_t = head_lanes(jnp.cos(ang_t))
    sin_t = head_lanes(jnp.sin(ang_t))
    lane = lax.broadcasted_iota(jnp.int32, (1, LANES), 1)
    q_scale = LOG2_E / math.sqrt(D_NOPE + D_ROPE)
    q_cos = jnp.where(lane < D_NOPE, q_scale, cos_t * q_scale)
    q_sin = sin_t * q_scale

    qn = _rms(q_c, gq_ref[...]).astype(BF16)
    qa = _dot(qn, wq1_ref[...])
    qb = _dot(qn, wq2_ref[...])
    kvn = _rms(kv_c, gkv_ref[...]).astype(BF16)
    kn = _dot(kvn, wk_ref[...])
    k_rope = kr * cos_t + kr_rot * sin_t
    for hd in range(HEADS):
        sl = slice(hd * HEAD_PAD, (hd + 1) * HEAD_PAD)
        q_out[0, hd] = (qa[:, sl] * q_cos + qb[:, sl] * q_sin).astype(BF16)
        k_out[0, hd] = (kn[:, sl] + k_rope).astype(BF16)

    vt = _dot_nt(wvt_ref[...], kvn)
    for j in range(tm // TK):
        vt_out[0, j] = vt[:, j * TK:(j + 1) * TK].astype(BF16)

    z = c_gate * u

    @pl.when(si == 0)
    def _():
        zbuf[0:SUBLANES, :] = jnp.zeros((SUBLANES, conv_dim), F32)

    zbuf[SUBLANES:SUBLANES + tm, :] = z
    z1 = zbuf[SUBLANES - 1:SUBLANES - 1 + tm, :]
    z2 = zbuf[SUBLANES - 2:SUBLANES - 2 + tm, :]
    cw = cw_ref[...]
    zc = cw[0:1] * z2 + cw[1:2] * z1 + cw[2:3] * z + cb_ref[...]
    conv = b_gate * zc
    conv_out[0] = _rms(conv, gc_ref[...]).astype(BF16)
    zbuf[0:SUBLANES, :] = zbuf[tm:tm + SUBLANES, :]


def _inproj(x, mod, pos, g1, win, gq, gkv, wq1, wq2, wk, wvt, invf, cw, cb, gc):
    bsz, seq, d = x.shape
    tm = TM_PROJ
    q_rank, kv_rank = wq1.shape[0], wk.shape[0]
    conv_dim = cw.shape[1]
    hv = wvt.shape[0]
    n_s = seq // tm
    const2 = lambda b, s: (0, 0)
    kern = functools.partial(_inproj_kernel, tm=tm, q_rank=q_rank, kv_rank=kv_rank,
                             conv_dim=conv_dim)
    return pl.pallas_call(
        kern,
        out_shape=(
            jax.ShapeDtypeStruct((bsz, HEADS, seq, HEAD_PAD), BF16),
            jax.ShapeDtypeStruct((bsz, HEADS, seq, HEAD_PAD), BF16),
            jax.ShapeDtypeStruct((bsz, seq // TK, hv, TK), BF16),
            jax.ShapeDtypeStruct((bsz, seq, conv_dim), BF16),
        ),
        grid=(bsz, n_s),
        in_specs=[
            pl.BlockSpec((1, tm, d), lambda b, s: (b, s, 0)),
            pl.BlockSpec((1, N_MOD, d), lambda b, s: (b, 0, 0)),
            pl.BlockSpec((1, 1, tm), lambda b, s: (b, 0, s)),
            pl.BlockSpec((1, d), const2),
            pl.BlockSpec(win.shape, const2),
            pl.BlockSpec((1, q_rank), const2),
            pl.BlockSpec((1, kv_rank), const2),
            pl.BlockSpec(wq1.shape, const2),
            pl.BlockSpec(wq2.shape, const2),
            pl.BlockSpec(wk.shape, const2),
            pl.BlockSpec(wvt.shape, const2),
            pl.BlockSpec(invf.shape, const2),
            pl.BlockSpec(cw.shape, const2),
            pl.BlockSpec((1, conv_dim), const2),
            pl.BlockSpec((1, conv_dim), const2),
        ],
        out_specs=(
            pl.BlockSpec((1, HEADS, tm, HEAD_PAD), lambda b, s: (b, 0, s, 0)),
            pl.BlockSpec((1, HEADS, tm, HEAD_PAD), lambda b, s: (b, 0, s, 0)),
            pl.BlockSpec((1, tm // TK, hv, TK), lambda b, s: (b, s, 0, 0)),
            pl.BlockSpec((1, tm, conv_dim), lambda b, s: (b, s, 0)),
        ),
        scratch_shapes=[pltpu.VMEM((tm + 2 * SUBLANES, conv_dim), F32)],
        compiler_params=pltpu.CompilerParams(
            dimension_semantics=("arbitrary", "arbitrary"), vmem_limit_bytes=VMEM_LIMIT),
        name="inproj",
    )(x, mod, pos, g1, win, gq, gkv, wq1, wq2, wk, wvt, invf, cw, cb, gc)


def _attn_kernel(q_ref, k_ref, vt_ref, o_ref, *, n_q):
    row = lax.broadcasted_iota(jnp.int32, (TK, TQ), 0)
    col = lax.broadcasted_iota(jnp.int32, (TK, TQ), 1)
    shift = CHUNK.bit_length() - 1
    k_per_q = TQ // TK
    ones_rows = jnp.ones((DEN_ROWS, TK), BF16)

    def scores(hh, qi, j):
        return _dot_nt(k_ref[0, hh, j * TK:(j + 1) * TK, :], q_ref[0, hh, qi * TQ:(qi + 1) * TQ, :])

    def softmax_pv(hh, qi, j, s_t, state):
        m, acc = state
        off = j - qi * k_per_q
        if off >= 0:
            ok = ((row + off * TK) >> shift) <= (col >> shift)
            s_t = jnp.where(ok, s_t, NEG_INF)
        m_new = jnp.maximum(m, jnp.max(s_t, axis=0, keepdims=True))
        p = jnp.exp2(s_t - m_new).astype(BF16)
        alpha = jnp.exp2(m - m_new)
        v_aug = jnp.concatenate([vt_ref[0, j, hh * D_V:(hh + 1) * D_V, :], ones_rows], axis=0)
        return m_new, alpha * acc + _dot(v_aug, p)

    steps = [(qi, j) for qi in range(n_q) for j in range((qi + 1) * k_per_q)]
    init = (jnp.full((1, TQ), NEG_INF, F32), jnp.zeros((D_V + DEN_ROWS, TQ), F32))
    s_cur = tuple(scores(hh, *steps[0]) for hh in range(2))
    states = None
    for idx, (qi, j) in enumerate(steps):
        s_next = None
        if idx + 1 < len(steps):
            s_next = tuple(scores(hh, *steps[idx + 1]) for hh in range(2))
        if j == 0:
            states = (init, init)
        states = tuple(softmax_pv(hh, qi, j, s_cur[hh], states[hh]) for hh in range(2))
        if j == (qi + 1) * k_per_q - 1:
            o_t = jnp.concatenate([acc[:D_V] / acc[D_V:D_V + 1] for (_, acc) in states],
                                  axis=0)
            o_ref[0, qi * TQ:(qi + 1) * TQ, :] = o_t.T.astype(BF16)
        s_cur = s_next


def _attention(q, k, vt):
    bsz, _, seq, _ = q.shape
    n_pair = HEADS // 2
    return pl.pallas_call(
        functools.partial(_attn_kernel, n_q=seq // TQ),
        out_shape=jax.ShapeDtypeStruct((bsz, seq, HEADS * D_V), BF16),
        grid=(bsz, n_pair),
        in_specs=[
            pl.BlockSpec((1, 2, seq, HEAD_PAD), lambda b, p: (b, p, 0, 0)),
            pl.BlockSpec((1, 2, seq, HEAD_PAD), lambda b, p: (b, p, 0, 0)),
            pl.BlockSpec((1, seq // TK, 2 * D_V, TK), lambda b, p: (b, 0, p, 0)),
        ],
        out_specs=pl.BlockSpec((1, seq, 2 * D_V), lambda b, p: (b, 0, p)),
        compiler_params=pltpu.CompilerParams(
            dimension_semantics=("arbitrary", "arbitrary"), vmem_limit_bytes=VMEM_LIMIT),
        name="attn",
    )(q, k, vt)


def _outproj_kernel(x_ref, attn_ref, conv_ref, mod_ref, ga_ref, wout_ref, g2_ref, wr_ref, br_ref,
                    tri_ref, x1_out, h2_out, route_out, rt_out, cnt_out, base_ref):
    first = (pl.program_id(0) == 0) & (pl.program_id(1) == 0)

    @pl.when(first)
    def _():
        base_ref[...] = jnp.zeros(base_ref.shape, F32)

    mod = mod_ref[0]
    an = _rms(attn_ref[0].astype(F32), ga_ref[...]).astype(BF16)
    mixed = jnp.concatenate([an, conv_ref[0]], axis=-1)
    y = _dot(mixed, wout_ref[...])
    x1 = x_ref[0] + mod[2:3] * y
    x1_out[0] = x1
    h2 = _rms(x1, g2_ref[...]) * (1.0 + mod[4:5]) + mod[3:4]
    _store_token_rows(h2_out, h2)
    logits = _dot(h2.astype(BF16), wr_ref[...]) + br_ref[...]

    tm = logits.shape[0]
    lane = lax.broadcasted_iota(jnp.int32, (tm, LANES), 1).astype(F32)
    big = float(LANES)
    is_grp = (lane >= N_EXPERTS) & (lane < N_EXPERTS + N_GROUPS)
    gl = jnp.where(is_grp, logits, -jnp.inf)
    gmax = jnp.max(gl, axis=-1, keepdims=True)
    gidx = jnp.min(jnp.where(gl == gmax, lane, big), axis=-1, keepdims=True) - N_EXPERTS
    p_group = 1.0 / jnp.sum(jnp.where(is_grp, jnp.exp(logits - gmax), 0.0), axis=-1, keepdims=True)
    lo = gidx * EXPERTS_PER_GROUP
    in_grp = (lane >= lo) & (lane < lo + EXPERTS_PER_GROUP)
    el = jnp.where(in_grp, logits, -jnp.inf)
    v1 = jnp.max(el, axis=-1, keepdims=True)
    i1 = jnp.min(jnp.where(el == v1, lane, big), axis=-1, keepdims=True)
    el2 = jnp.where(lane == i1, -jnp.inf, el)
    v2 = jnp.max(el2, axis=-1, keepdims=True)
    i2 = jnp.min(jnp.where(el2 == v2, lane, big), axis=-1, keepdims=True)
    t = jnp.exp(v2 - v1)
    w1 = p_group / (1.0 + t)
    w2 = p_group * t / (1.0 + t)

    oh1 = lane == i1
    oh2 = lane == i2
    cnt = jnp.where(oh1 | oh2, 1.0, 0.0)
    tri_n = tri_ref.shape[0]
    cnt_bf = cnt.astype(BF16)
    run = base_ref[0:1, :]
    parts = []
    for sb in range(tm // tri_n):
        rows = slice(sb * tri_n, (sb + 1) * tri_n)
        parts.append(_dot(tri_ref[...], cnt_bf[rows]) + run)
        run = run + jnp.sum(cnt[rows], axis=0, keepdims=True)
    prefix = jnp.concatenate(parts, axis=0)
    r1 = jnp.sum(jnp.where(oh1, prefix, 0.0), axis=-1, keepdims=True)
    r2 = jnp.sum(jnp.where(oh2, prefix, 0.0), axis=-1, keepdims=True)
    new_base = base_ref[...] + jnp.sum(cnt, axis=0, keepdims=True)
    base_ref[...] = new_base
    cnt_out[...] = new_base

    route = jnp.where(lane == 0.0, i1, jnp.where(lane == 1.0, i2, jnp.where(
        lane == 2.0, r1, jnp.where(lane == 3.0, r2, jnp.where(
            lane == 4.0, w1, jnp.where(lane == 5.0, w2, 0.0))))))
    route_out[0] = route
    rt_out[...] = route.T[0:SUBLANES, :].astype(jnp.int32)


def _outproj(x, attn, conv, mod, ga, wout, g2, wr, br):
    bsz, seq, d = x.shape
    tm = TM_OUT
    aw = attn.shape[-1]
    cwd = conv.shape[-1]
    n_s = seq // tm
    rpt = d // LANES
    tri_n = min(tm, TRI)
    tri = (lax.broadcasted_iota(jnp.int32, (tri_n, tri_n), 1)
           < lax.broadcasted_iota(jnp.int32, (tri_n, tri_n), 0)).astype(BF16)
    const2 = lambda b, s: (0, 0)
    tile3 = lambda b, s: (b, s, 0)
    return pl.pallas_call(
        _outproj_kernel,
        out_shape=(
            jax.ShapeDtypeStruct((bsz, seq, d), F32),
            jax.ShapeDtypeStruct((bsz * seq * rpt, LANES), F32),
            jax.ShapeDtypeStruct((bsz, seq, LANES), F32),
            jax.ShapeDtypeStruct((SUBLANES, bsz * seq), jnp.int32),
            jax.ShapeDtypeStruct((SUBLANES, LANES), F32),
        ),
        grid=(bsz, n_s),
        in_specs=[
            pl.BlockSpec((1, tm, d), tile3),
            pl.BlockSpec((1, tm, aw), tile3),
            pl.BlockSpec((1, tm, cwd), tile3),
            pl.BlockSpec((1, N_MOD, d), lambda b, s: (b, 0, 0)),
            pl.BlockSpec((1, aw), const2),
            pl.BlockSpec(wout.shape, const2),
            pl.BlockSpec((1, d), const2),
            pl.BlockSpec(wr.shape, const2),
            pl.BlockSpec((1, LANES), const2),
            pl.BlockSpec((tri_n, tri_n), const2),
        ],
        out_specs=(
            pl.BlockSpec((1, tm, d), tile3),
            pl.BlockSpec((tm * rpt, LANES), lambda b, s: (b * n_s + s, 0)),
            pl.BlockSpec((1, tm, LANES), tile3),
            pl.BlockSpec((SUBLANES, tm), lambda b, s: (0, b * n_s + s)),
            pl.BlockSpec((SUBLANES, LANES), const2),
        ),
        scratch_shapes=[pltpu.VMEM((SUBLANES, LANES), F32)],
        compiler_params=pltpu.CompilerParams(
            dimension_semantics=("arbitrary", "arbitrary"), vmem_limit_bytes=VMEM_LIMIT),
        name="outproj",
    )(x, attn, conv, mod, ga, wout, g2, wr, br, tri)


def _dest_kernel(starts_ref, rt_ref, dest_out):
    rt = rt_ref[...]
    ex = rt[0:TOP_K, :]
    start = jnp.zeros(ex.shape, jnp.int32)
    for e in range(N_EXPERTS):
        start = jnp.where(ex == e, starts_ref[e], start)
    dest = start + rt[TOP_K:2 * TOP_K, :]
    dest_out[...] = jnp.concatenate(
        [dest, jnp.zeros((SUBLANES - TOP_K, dest.shape[1]), jnp.int32)], axis=0)


def _dest(starts, rt):
    t = rt.shape[1]
    tm = TM_DEST
    grid_spec = pltpu.PrefetchScalarGridSpec(
        num_scalar_prefetch=1,
        grid=(t // tm,),
        in_specs=[pl.BlockSpec((SUBLANES, tm), lambda i, st: (0, i))],
        out_specs=pl.BlockSpec((SUBLANES, tm), lambda i, st: (0, i)),
    )
    return pl.pallas_call(
        _dest_kernel,
        out_shape=jax.ShapeDtypeStruct((SUBLANES, t), jnp.int32),
        grid_spec=grid_spec,
        compiler_params=pltpu.CompilerParams(
            dimension_semantics=("arbitrary",), vmem_limit_bytes=VMEM_LIMIT),
        name="dest",
    )(starts, rt)


def _token_copy(src, src_tok, dst, dst_tok, sem, rpt):
    return pltpu.make_async_copy(src.at[pl.ds(src_tok * rpt, rpt)], dst.at[pl.ds(dst_tok * rpt, rpt)],
                                 sem)


def _dispatch_kernel(dest_ref, h_hbm, xs_hbm, hbuf, sem_in, sem_out, *, tm, rpt):
    i = pl.program_id(0)
    n = pl.num_programs(0)
    slot = i % DISP_SLOTS

    def tile_load(tile, slot_idx):
        return pltpu.make_async_copy(h_hbm.at[pl.ds(tile * tm * rpt, tm * rpt)], hbuf.at[slot_idx],
                                     sem_in.at[slot_idx])

    def drain(slot_idx):
        def body(t, carry):
            for k in range(TOP_K):
                _token_copy(hbuf.at[slot_idx], 0, xs_hbm, 0, sem_out.at[slot_idx], rpt).wait()
            return carry
        lax.fori_loop(0, tm, body, 0, unroll=8)

    @pl.when(i == 0)
    def _():
        tile_load(0, 0).start()

    @pl.when((i == 0) & (n > 1))
    def _():
        tile_load(1, 1).start()

    tile_load(i, slot).wait()
    for t in range(tm):
        for k in range(TOP_K):
            _token_copy(hbuf.at[slot], t, xs_hbm, dest_ref[k, t], sem_out.at[slot], rpt).start(
                priority=k)

    refill = (i + 2) % DISP_SLOTS

    @pl.when(i > 0)
    def _():
        drain(refill)

    @pl.when(i + 2 < n)
    def _():
        tile_load(i + 2, refill).start()

    @pl.when(i == n - 1)
    def _():
        drain(slot)


def _dispatch(dest, h, rpt):
    t = h.shape[0] // rpt
    tm = TM_DISP
    return pl.pallas_call(
        functools.partial(_dispatch_kernel, tm=tm, rpt=rpt),
        out_shape=jax.ShapeDtypeStruct((t * TOP_K * rpt, LANES), h.dtype),
        grid=(t // tm,),
        in_specs=[
            pl.BlockSpec((SUBLANES, tm), lambda i: (0, i), memory_space=pltpu.SMEM),
            pl.BlockSpec(memory_space=pl.ANY),
        ],
        out_specs=pl.BlockSpec(memory_space=pl.ANY),
        scratch_shapes=[pltpu.VMEM((DISP_SLOTS, tm * rpt, LANES), h.dtype),
                        pltpu.SemaphoreType.DMA((DISP_SLOTS,)),
                        pltpu.SemaphoreType.DMA((DISP_SLOTS,))],
        compiler_params=pltpu.CompilerParams(
            dimension_semantics=("arbitrary",), vmem_limit_bytes=VMEM_LIMIT),
        name="dispatch",
    )(dest, h)


def _experts_kernel(blk_ref, exp_ref, lo_ref, hi_ref, x_ref, wg_ref, wu_ref, wd_ref, o_ref, acc_ref,
                    wgu_bf, wd_bf, *, n_items, rpt):
    i = pl.program_id(0)
    lo = lo_ref[i]
    hi = hi_ref[i]
    blk = blk_ref[i]
    first = (i == 0) | (blk != blk_ref[jnp.maximum(i - 1, 0)])

    @pl.when(i == 0)
    def _():
        acc_ref[...] = jnp.zeros(acc_ref.shape, F32)

    @pl.when((i == 0) | (exp_ref[i] != exp_ref[jnp.maximum(i - 1, 0)]))
    def _():
        de = wd_ref.shape[1]
        wgu_bf[:, :de] = wg_ref[0].astype(BF16)
        wgu_bf[:, de:] = wu_ref[0].astype(BF16)
        wd_bf[...] = wd_ref[0].astype(BF16)

    @pl.when(hi > lo)
    def _():
        for sb in range(BLK // SUB):
            rows = pl.ds(sb * SUB, SUB)
            tok_rows = pl.ds(sb * SUB * rpt, SUB * rpt)
            x = _load_token_rows(x_ref.at[tok_rows], SUB, rpt).astype(BF16)
            gu = _dot(x, wgu_bf[...])
            de = gu.shape[1] // 2
            g = gu[:, :de]
            a = (g * jax.nn.sigmoid(g) * gu[:, de:]).astype(BF16)
            y = _dot(a, wd_bf[...])
            row = lax.broadcasted_iota(jnp.int32, (SUB, 1), 0) + sb * SUB
            keep = (row >= lo) & (row < hi)
            acc = jnp.where(first, 0.0, acc_ref[rows, :]) + jnp.where(keep, y, 0.0)
            acc_ref[rows, :] = acc
            _store_token_rows(o_ref.at[tok_rows], acc)


def _experts(item_blk, item_exp, item_lo, item_hi, xs, wg, wu, wd, rpt):
    a = xs.shape[0] // rpt
    d = rpt * LANES
    de = wd.shape[1]
    n_items = item_blk.shape[0]
    grid_spec = pltpu.PrefetchScalarGridSpec(
        num_scalar_prefetch=4,
        grid=(n_items,),
        in_specs=[
            pl.BlockSpec((BLK * rpt, LANES), lambda i, blk, ex, lo, hi: (blk[i], 0)),
            pl.BlockSpec((1, d, de), lambda i, blk, ex, lo, hi: (ex[i], 0, 0)),
            pl.BlockSpec((1, d, de), lambda i, blk, ex, lo, hi: (ex[i], 0, 0)),
            pl.BlockSpec((1, de, d), lambda i, blk, ex, lo, hi: (ex[i], 0, 0)),
        ],
        out_specs=pl.BlockSpec((BLK * rpt, LANES), lambda i, blk, ex, lo, hi: (blk[i], 0)),
        scratch_shapes=[pltpu.VMEM((BLK, d), F32), pltpu.VMEM((d, 2 * de), BF16),
                        pltpu.VMEM((de, d), BF16)],
    )
    return pl.pallas_call(
        functools.partial(_experts_kernel, n_items=n_items, rpt=rpt),
        out_shape=jax.ShapeDtypeStruct((a * rpt, LANES), F32),
        grid_spec=grid_spec,
        compiler_params=pltpu.CompilerParams(
            dimension_semantics=("arbitrary",), vmem_limit_bytes=VMEM_LIMIT),
        name="experts",
    )(item_blk, item_exp, item_lo, item_hi, xs, wg, wu, wd)


def _work_items(counts, n_rows):
    n_blocks = n_rows // BLK
    n_items = n_blocks + N_EXPERTS - 1
    counts = counts.astype(jnp.int32)
    ends = jnp.cumsum(counts)
    starts = ends - counts
    first_blk = starts // BLK
    last_blk = jnp.maximum(ends - 1, 0) // BLK
    n_e = jnp.where(counts > 0, last_blk - first_blk + 1, 0)
    item_end = jnp.cumsum(n_e)
    item_start = item_end - n_e
    idx = jnp.arange(n_items, dtype=jnp.int32)
    total = item_end[-1]
    ex = jnp.minimum(jnp.sum((idx[:, None] >= item_end[None, :]).astype(jnp.int32), axis=1),
                     N_EXPERTS - 1)
    onehot = (ex[:, None] == jnp.arange(N_EXPERTS, dtype=jnp.int32)[None, :]).astype(jnp.int32)
    pick = lambda table: jnp.sum(onehot * table[None, :], axis=1)
    blk = pick(first_blk) + (idx - pick(item_start))
    lo = jnp.maximum(pick(starts), blk * BLK) - blk * BLK
    hi = jnp.minimum(pick(ends), (blk + 1) * BLK) - blk * BLK
    valid = idx < total
    last_ex = jnp.max(jnp.where(valid, ex, 0))
    blk = jnp.where(valid, blk, n_blocks - 1).astype(jnp.int32)
    ex = jnp.where(valid, ex, last_ex).astype(jnp.int32)
    lo = jnp.where(valid, lo, 0).astype(jnp.int32)
    hi = jnp.where(valid, hi, 0).astype(jnp.int32)
    return blk, ex, lo, hi


def _combine_kernel(dest_ref, dest_one_ref, dest_ahead_ref, ys_hbm, x1_ref, route_ref, mod_ref,
                    gf_ref, o_ref, ybuf, sems, *, tm, final_norm, rpt):
    i = pl.program_id(0)
    slot = i % N_SLOTS

    def gather(d_ref, slot_idx):
        for t in range(tm):
            for k in range(TOP_K):
                _token_copy(ys_hbm, d_ref[k, t], ybuf.at[slot_idx, k], t, sems.at[slot_idx],
                            rpt).start(priority=k)

    def drain(slot_idx):
        def body(t, carry):
            for k in range(TOP_K):
                _token_copy(ys_hbm, 0, ybuf.at[slot_idx, k], 0, sems.at[slot_idx], rpt).wait()
            return carry
        lax.fori_loop(0, tm, body, 0, unroll=8)

    @pl.when(i == 0)
    def _():
        gather(dest_ref, 0)
        gather(dest_one_ref, 1)

    drain(slot)
    r = route_ref[...]
    y = (r[:, 4:5] * _load_token_rows(ybuf.at[slot, 0], tm, rpt)
         + r[:, 5:6] * _load_token_rows(ybuf.at[slot, 1], tm, rpt))
    mod = mod_ref[0]
    x2 = x1_ref[...] + mod[5:6] * y
    o_ref[...] = _rms(x2, gf_ref[...]) if final_norm else x2

    gather(dest_ahead_ref, (i + GATHER_AHEAD) % N_SLOTS)

    @pl.when(i == pl.num_programs(0) - 1)
    def _():
        for extra in range(1, N_SLOTS):
            drain((i + extra) % N_SLOTS)


def _combine(dest, ys, x1, route, mod, gf, seq, final_norm):
    t, d = x1.shape
    tm = TM_COMB
    n_steps = t // tm
    per_b = seq // tm
    rpt = d // LANES
    return pl.pallas_call(
        functools.partial(_combine_kernel, tm=tm, final_norm=final_norm, rpt=rpt),
        out_shape=jax.ShapeDtypeStruct((t, d), F32),
        grid=(n_steps,),
        in_specs=[
            pl.BlockSpec((SUBLANES, tm), lambda i: (0, i), memory_space=pltpu.SMEM),
            pl.BlockSpec((SUBLANES, tm), lambda i: (0, min(1, n_steps - 1)),
                         memory_space=pltpu.SMEM),
            pl.BlockSpec((SUBLANES, tm), lambda i: (0, jnp.minimum(i + GATHER_AHEAD, n_steps - 1)),
                         memory_space=pltpu.SMEM),
            pl.BlockSpec(memory_space=pl.ANY),
            pl.BlockSpec((tm, d), lambda i: (i, 0)),
            pl.BlockSpec((tm, LANES), lambda i: (i, 0)),
            pl.BlockSpec((1, N_MOD, d), lambda i: (i // per_b, 0, 0)),
            pl.BlockSpec((1, d), lambda i: (0, 0)),
        ],
        out_specs=pl.BlockSpec((tm, d), lambda i: (i, 0)),
        scratch_shapes=[pltpu.VMEM((N_SLOTS, TOP_K, tm * rpt, LANES), F32),
                        pltpu.SemaphoreType.DMA((N_SLOTS,))],
        compiler_params=pltpu.CompilerParams(
            dimension_semantics=("arbitrary",), vmem_limit_bytes=VMEM_LIMIT),
        name="combine",
    )(dest, dest, dest, ys, x1, route, mod, gf)


def _prep_in_weights(w_in, q_rank, kv_rank, conv_dim):
    o1 = q_rank
    o2 = o1 + kv_rank
    o3 = o2 + D_ROPE
    d = w_in.shape[0]
    w_kr = w_in[:, o2:o3]
    half = D_ROPE // 2
    w_kr_rot = jnp.concatenate([-w_kr[:, half:], w_kr[:, :half]], axis=1)
    zl = jnp.zeros((d, D_NOPE), w_in.dtype)
    zr = jnp.zeros((d, LANES - D_NOPE - D_ROPE), w_in.dtype)
    return jnp.concatenate(
        [w_in[:, :o2], zl, w_kr, zr, zl, w_kr_rot, zr, w_in[:, o3:]], axis=1).astype(BF16)


def _prep_q_weights(w_uq):
    r = w_uq.shape[0]
    w = w_uq.reshape(r, HEADS, D_NOPE + D_ROPE)
    nope, rope = w[..., :D_NOPE], w[..., D_NOPE:]
    half = D_ROPE // 2
    rot = jnp.concatenate([-rope[..., half:], rope[..., :half]], axis=-1)
    zr = jnp.zeros((r, HEADS, HEAD_PAD - D_NOPE - D_ROPE), w_uq.dtype)
    w1 = jnp.concatenate([nope, rope, zr], axis=-1).reshape(r, HEADS * HEAD_PAD)
    w2 = jnp.concatenate([jnp.zeros_like(nope), rot, zr], axis=-1).reshape(r, HEADS * HEAD_PAD)
    return w1.astype(BF16), w2.astype(BF16)


def _prep_k_weights(w_uk):
    r = w_uk.shape[0]
    w = w_uk.reshape(r, HEADS, D_NOPE)
    zr = jnp.zeros((r, HEADS, HEAD_PAD - D_NOPE), w_uk.dtype)
    return jnp.concatenate([w, zr], axis=-1).reshape(r, HEADS * HEAD_PAD).astype(BF16)


def _rope_freqs():
    inv_freq = 1.0 / (ROPE_THETA ** (jnp.arange(0, D_ROPE, 2, dtype=F32) / D_ROPE))
    return inv_freq.reshape(D_ROPE // 2, 1)


def kernel(x, c, positions, w_ada, b_ada, norm1_g, w_in, q_norm_g, kv_norm_g, w_uq, w_uk, w_uv,
           conv_w, conv_b, attn_out_g, conv_out_g, w_out, norm2_g, w_router_group, b_router_group,
           w_router_expert, b_router_expert, w_exp_gate, w_exp_up, w_exp_down, final_g):
    bsz, seq, d = x.shape
    depth = w_ada.shape[0]
    q_rank = q_norm_g.shape[-1]
    kv_rank = kv_norm_g.shape[-1]
    conv_dim = conv_w.shape[-1]
    t = bsz * seq
    assert seq % max(TM_PROJ, TQ, TM_OUT, TM_COMB) == 0 and TQ % TK == 0 and TM_PROJ % TK == 0
    assert CHUNK & (CHUNK - 1) == 0 and TQ % CHUNK == 0
    assert t % max(TM_DEST, TM_DISP, TM_COMB) == 0 and (t * TOP_K) % BLK == 0
    assert w_uq.shape[-1] == HEADS * (D_NOPE + D_ROPE) and w_uv.shape[-1] == HEADS * D_V

    pos = positions.reshape(bsz, 1, seq)
    invf = _rope_freqs()
    row = lambda v: v.reshape(1, -1)
    for l in range(depth):
        mod = _adaln(c, w_ada[l], b_ada[l]).reshape(bsz, N_MOD, d)
        win = _prep_in_weights(w_in[l], q_rank, kv_rank, conv_dim)
        wq1, wq2 = _prep_q_weights(w_uq[l])
        wk = _prep_k_weights(w_uk[l])
        wvt = w_uv[l].T.astype(BF16)
        q, k, vt, conv = _inproj(x, mod, pos, row(norm1_g[l]), win, row(q_norm_g[l]),
                                 row(kv_norm_g[l]), wq1, wq2, wk, wvt, invf, conv_w[l],
                                 row(conv_b[l]), row(conv_out_g[l]))
        attn = _attention(q, k, vt)

        wr = jnp.concatenate(
            [w_router_expert[l], w_router_group[l],
             jnp.zeros((d, LANES - N_EXPERTS - N_GROUPS), F32)], axis=1).astype(BF16)
        br = jnp.concatenate(
            [b_router_expert[l], b_router_group[l],
             jnp.zeros((LANES - N_EXPERTS - N_GROUPS,), F32)]).reshape(1, LANES)
        x1, h2, route, rt, cnt = _outproj(x, attn, conv, mod, row(attn_out_g[l]),
                                      w_out[l].astype(BF16), row(norm2_g[l]), wr, br)
        route = route.reshape(t, LANES)
        counts = cnt[0, :N_EXPERTS].astype(jnp.int32)
        starts = jnp.cumsum(counts) - counts
        rpt = d // LANES
        dest = _dest(starts, rt)
        xs = _dispatch(dest, h2, rpt)
        items = _work_items(counts, t * TOP_K)
        ys = _experts(*items, xs, w_exp_gate[l], w_exp_up[l], w_exp_down[l], rpt)
        out = _combine(dest, ys, x1.reshape(t, d), route, mod, row(final_g), seq, l == depth - 1)
        x = out.reshape(bsz, seq, d)
    return x
```

```python
import functools
import math

import jax
import jax.numpy as jnp
from jax import lax
from jax.experimental import pallas as pl
from jax.experimental.pallas import tpu as pltpu

F32 = jnp.float32
BF16 = jnp.bfloat16

CHUNK = 64
HEADS = 8
D_NOPE = 64
D_ROPE = 32
D_V = 64
ROPE_THETA = 10000.0
CONV_WIDTH = 3
N_GROUPS = 4
EXPERTS_PER_GROUP = 8
N_EXPERTS = N_GROUPS * EXPERTS_PER_GROUP
TOP_K = 2
N_MOD = 6
RMS_EPS = 1e-6
NEG_INF = -1e30
LOG2_E = math.log2(math.e)

LANES = 128
SUBLANES = 8
HEAD_PAD = LANES
DEN_ROWS = 16
VMEM_LIMIT = 56 * 1024 * 1024

TM_PROJ = 512
TQ = 512
TK = 256
TM_OUT = 1024
TM_DEST = 2048
TM_DISP = 512
BLK = 512
SUB = 256
TM_COMB = 512
TN_ADA = 512
TRI = 256
GATHER_AHEAD = 2
N_SLOTS = GATHER_AHEAD + 1
DISP_SLOTS = 3


def _rms(x, g):
    return x * lax.rsqrt(jnp.mean(x * x, axis=-1, keepdims=True) + RMS_EPS) * g


def _dot(a, b):
    return jnp.dot(a, b, preferred_element_type=F32)


def _dot_nt(a, b):
    return lax.dot_general(a, b, (((1,), (1,)), ((), ())), preferred_element_type=F32)


def _store_token_rows(ref, x):
    n, d = x.shape
    r = d // LANES
    for j in range(r):
        ref[pl.ds(j, n, stride=r), :] = x[:, j * LANES:(j + 1) * LANES]


def _load_token_rows(ref, n, r):
    return jnp.concatenate([ref[pl.ds(j, n, stride=r), :] for j in range(r)], axis=1)


def _split_bf16(x):
    hi = x.astype(BF16)
    lo = (x - hi.astype(F32)).astype(BF16)
    return hi, lo


def _adaln_kernel(c_ref, w_ref, b_ref, o_ref):
    c = c_ref[...]
    ca = c * jax.nn.sigmoid(c)
    c_hi, c_lo = _split_bf16(ca)
    w_hi, w_lo = _split_bf16(w_ref[...])
    acc = _dot(c_hi, w_hi) + _dot(c_hi, w_lo) + _dot(c_lo, w_hi)
    o_ref[...] = acc + b_ref[...]


def _adaln(c, w, b):
    bsz, d = c.shape
    n = w.shape[1]
    return pl.pallas_call(
        _adaln_kernel,
        out_shape=jax.ShapeDtypeStruct((bsz, n), F32),
        grid=(n // TN_ADA,),
        in_specs=[
            pl.BlockSpec((bsz, d), lambda j: (0, 0)),
            pl.BlockSpec((d, TN_ADA), lambda j: (0, j)),
            pl.BlockSpec((1, TN_ADA), lambda j: (0, j)),
        ],
        out_specs=pl.BlockSpec((bsz, TN_ADA), lambda j: (0, j)),
        compiler_params=pltpu.CompilerParams(
            dimension_semantics=("arbitrary",), vmem_limit_bytes=VMEM_LIMIT),
        name="adaln",
    )(c, w, b.reshape(1, n))


def _inproj_kernel(x_ref, mod_ref, pos_ref, g1_ref, win_ref, gq_ref, gkv_ref, wq1_ref, wq2_ref,
                   wk_ref, wvt_ref, invf_ref, cw_ref, cb_ref, gc_ref,
                   q_out, k_out, vt_out, conv_out, zbuf, *, tm, q_rank, kv_rank, conv_dim):
    si = pl.program_id(1)
    x = x_ref[0]
    mod = mod_ref[0]
    h = (_rms(x, g1_ref[...]) * (1.0 + mod[1:2]) + mod[0:1]).astype(BF16)
    proj = _dot(h, win_ref[...])

    o_kv = q_rank
    o_kr = o_kv + kv_rank
    o_u = o_kr + 2 * LANES
    q_c = proj[:, 0:q_rank]
    kv_c = proj[:, o_kv:o_kv + kv_rank]
    kr = proj[:, o_kr:o_kr + LANES]
    kr_rot = proj[:, o_kr + LANES:o_kr + 2 * LANES]
    u = proj[:, o_u:o_u + conv_dim]
    b_gate = proj[:, o_u + conv_dim:o_u + 2 * conv_dim]
    c_gate = proj[:, o_u + 2 * conv_dim:o_u + 3 * conv_dim]

    ang_t = invf_ref[...] * pos_ref[0].astype(F32)

    def head_lanes(tab_t):
        zl = jnp.zeros((D_NOPE, tm), F32)
        zr = jnp.zeros((LANES - D_NOPE - D_ROPE, tm), F32)
        return jnp.concatenate([zl, tab_t, tab_t, zr], axis=0).T

    cos_t = head_lanes(jnp.cos(ang_t))
    sin_t = head_lanes(jnp.sin(ang_t))
    lane = lax.broadcasted_iota(jnp.int32, (1, LANES), 1)
    q_scale = LOG2_E / math.sqrt(D_NOPE + D_ROPE)
    q_cos = jnp.where(lane < D_NOPE, q_scale, cos_t * q_scale)
    q_sin = sin_t * q_scale

    qn = _rms(q_c, gq_ref[...]).astype(BF16)
    qa = _dot(qn, wq1_ref[...])
    qb = _dot(qn, wq2_ref[...])
    kvn = _rms(kv_c, gkv_ref[...]).astype(BF16)
    kn = _dot(kvn, wk_ref[...])
    k_rope = kr * cos_t + kr_rot * sin_t
    for hd in range(HEADS):
        sl = slice(hd * HEAD_PAD, (hd + 1) * HEAD_PAD)
        q_out[0, hd] = (qa[:, sl] * q_cos + qb[:, sl] * q_sin).astype(BF16)
        k_out[0, hd] = (kn[:, sl] + k_rope).astype(BF16)

    vt = _dot_nt(wvt_ref[...], kvn)
    for j in range(tm // TK):
        vt_out[0, j] = vt[:, j * TK:(j + 1) * TK].astype(BF16)

    z = c_gate * u

    @pl.when(si == 0)
    def _():
        zbuf[0:SUBLANES, :] = jnp.zeros((SUBLANES, conv_dim), F32)

    zbuf[SUBLANES:SUBLANES + tm, :] = z
    z1 = zbuf[SUBLANES - 1:SUBLANES - 1 + tm, :]
    z2 = zbuf[SUBLANES - 2:SUBLANES - 2 + tm, :]
    cw = cw_ref[...]
    zc = cw[0:1] * z2 + cw[1:2] * z1 + cw[2:3] * z + cb_ref[...]
    conv = b_gate * zc
    conv_out[0] = _rms(conv, gc_ref[...]).astype(BF16)
    zbuf[0:SUBLANES, :] = zbuf[tm:tm + SUBLANES, :]


def _inproj(x, mod, pos, g1, win, gq, gkv, wq1, wq2, wk, wvt, invf, cw, cb, gc):
    bsz, seq, d = x.shape
    tm = TM_PROJ
    q_rank, kv_rank = wq1.shape[0], wk.shape[0]
    conv_dim = cw.shape[1]
    hv = wvt.shape[0]
    n_s = seq // tm
    const2 = lambda b, s: (0, 0)
    kern = functools.partial(_inproj_kernel, tm=tm, q_rank=q_rank, kv_rank=kv_rank,
                             conv_dim=conv_dim)
    return pl.pallas_call(
        kern,
        out_shape=(
            jax.ShapeDtypeStruct((bsz, HEADS, seq, HEAD_PAD), BF16),
            jax.ShapeDtypeStruct((bsz, HEADS, seq, HEAD_PAD), BF16),
            jax.ShapeDtypeStruct((bsz, seq // TK, hv, TK), BF16),
            jax.ShapeDtypeStruct((bsz, seq, conv_dim), BF16),
        ),
        grid=(bsz, n_s),
        in_specs=[
            pl.BlockSpec((1, tm, d), lambda b, s: (b, s, 0)),
            pl.BlockSpec((1, N_MOD, d), lambda b, s: (b, 0, 0)),
            pl.BlockSpec((1, 1, tm), lambda b, s: (b, 0, s)),
            pl.BlockSpec((1, d), const2),
            pl.BlockSpec(win.shape, const2),
            pl.BlockSpec((1, q_rank), const2),
            pl.BlockSpec((1, kv_rank), const2),
            pl.BlockSpec(wq1.shape, const2),
            pl.BlockSpec(wq2.shape, const2),
            pl.BlockSpec(wk.shape, const2),
            pl.BlockSpec(wvt.shape, const2),
            pl.BlockSpec(invf.shape, const2),
            pl.BlockSpec(cw.shape, const2),
            pl.BlockSpec((1, conv_dim), const2),
            pl.BlockSpec((1, conv_dim), const2),
        ],
        out_specs=(
            pl.BlockSpec((1, HEADS, tm, HEAD_PAD), lambda b, s: (b, 0, s, 0)),
            pl.BlockSpec((1, HEADS, tm, HEAD_PAD), lambda b, s: (b, 0, s, 0)),
            pl.BlockSpec((1, tm // TK, hv, TK), lambda b, s: (b, s, 0, 0)),
            pl.BlockSpec((1, tm, conv_dim), lambda b, s: (b, s, 0)),
        ),
        scratch_shapes=[pltpu.VMEM((tm + 2 * SUBLANES, conv_dim), F32)],
        compiler_params=pltpu.CompilerParams(
            dimension_semantics=("arbitrary", "arbitrary"), vmem_limit_bytes=VMEM_LIMIT),
        name="inproj",
    )(x, mod, pos, g1, win, gq, gkv, wq1, wq2, wk, wvt, invf, cw, cb, gc)


def _attn_kernel(q_ref, k_ref, vt_ref, o_ref, *, n_q):
    row = lax.broadcasted_iota(jnp.int32, (TK, TQ), 0)
    col = lax.broadcasted_iota(jnp.int32, (TK, TQ), 1)
    shift = CHUNK.bit_length() - 1
    k_per_q = TQ // TK
    ones_rows = jnp.ones((DEN_ROWS, TK), BF16)

    def scores(hh, qi, j):
        return _dot_nt(k_ref[0, hh, j * TK:(j + 1) * TK, :], q_ref[0, hh, qi * TQ:(qi + 1) * TQ, :])

    def softmax_pv(hh, qi, j, s_t, state):
        m, acc = state
        off = j - qi * k_per_q
        if off >= 0:
            ok = ((row + off * TK) >> shift) <= (col >> shift)
            s_t = jnp.where(ok, s_t, NEG_INF)
        m_new = jnp.maximum(m, jnp.max(s_t, axis=0, keepdims=True))
        p = jnp.exp2(s_t - m_new).astype(BF16)
        alpha = jnp.exp2(m - m_new)
        v_aug = jnp.concatenate([vt_ref[0, j, hh * D_V:(hh + 1) * D_V, :], ones_rows], axis=0)
        return m_new, alpha * acc + _dot(v_aug, p)

    steps = [(qi, j) for qi in range(n_q) for j in range((qi + 1) * k_per_q)]
    init = (jnp.full((1, TQ), NEG_INF, F32), jnp.zeros((D_V + DEN_ROWS, TQ), F32))
    s_cur = tuple(scores(hh, *steps[0]) for hh in range(2))
    states = None
    for idx, (qi, j) in enumerate(steps):
        s_next = None
        if idx + 1 < len(steps):
            s_next = tuple(scores(hh, *steps[idx + 1]) for hh in range(2))
        if j == 0:
            states = (init, init)
        states = tuple(softmax_pv(hh, qi, j, s_cur[hh], states[hh]) for hh in range(2))
        if j == (qi + 1) * k_per_q - 1:
            o_t = jnp.concatenate([acc[:D_V] / acc[D_V:D_V + 1] for (_, acc) in states],
                                  axis=0)
            o_ref[0, qi * TQ:(qi + 1) * TQ, :] = o_t.T.astype(BF16)
        s_cur = s_next


def _attention(q, k, vt):
    bsz, _, seq, _ = q.shape
    n_pair = HEADS // 2
    return pl.pallas_call(
        functools.partial(_attn_kernel, n_q=seq // TQ),
        out_shape=jax.ShapeDtypeStruct((bsz, seq, HEADS * D_V), BF16),
        grid=(bsz, n_pair),
        in_specs=[
            pl.BlockSpec((1, 2, seq, HEAD_PAD), lambda b, p: (b, p, 0, 0)),
            pl.BlockSpec((1, 2, seq, HEAD_PAD), lambda b, p: (b, p, 0, 0)),
            pl.BlockSpec((1, seq // TK, 2 * D_V, TK), lambda b, p: (b, 0, p, 0)),
        ],
        out_specs=pl.BlockSpec((1, seq, 2 * D_V), lambda b, p: (b, 0, p)),
        compiler_params=pltpu.CompilerParams(
            dimension_semantics=("arbitrary", "arbitrary"), vmem_limit_bytes=VMEM_LIMIT),
        name="attn",
    )(q, k, vt)


def _outproj_kernel(x_ref, attn_ref, conv_ref, mod_ref, ga_ref, wout_ref, g2_ref, wr_ref, br_ref,
                    tri_ref, x1_out, h2_out, route_out, rt_out, cnt_out, base_ref):
    first = (pl.program_id(0) == 0) & (pl.program_id(1) == 0)

    @pl.when(first)
    def _():
        base_ref[...] = jnp.zeros(base_ref.shape, F32)

    mod = mod_ref[0]
    an = _rms(attn_ref[0].astype(F32), ga_ref[...]).astype(BF16)
    mixed = jnp.concatenate([an, conv_ref[0]], axis=-1)
    y = _dot(mixed, wout_ref[...])
    x1 = x_ref[0] + mod[2:3] * y
    x1_out[0] = x1
    h2 = _rms(x1, g2_ref[...]) * (1.0 + mod[4:5]) + mod[3:4]
    _store_token_rows(h2_out, h2)
    logits = _dot(h2.astype(BF16), wr_ref[...]) + br_ref[...]

    tm = logits.shape[0]
    lane = lax.broadcasted_iota(jnp.int32, (tm, LANES), 1).astype(F32)
    big = float(LANES)
    is_grp = (lane >= N_EXPERTS) & (lane < N_EXPERTS + N_GROUPS)
    gl = jnp.where(is_grp, logits, -jnp.inf)
    gmax = jnp.max(gl, axis=-1, keepdims=True)
    gidx = jnp.min(jnp.where(gl == gmax, lane, big), axis=-1, keepdims=True) - N_EXPERTS
    p_group = 1.0 / jnp.sum(jnp.where(is_grp, jnp.exp(logits - gmax), 0.0), axis=-1, keepdims=True)
    lo = gidx * EXPERTS_PER_GROUP
    in_grp = (lane >= lo) & (lane < lo + EXPERTS_PER_GROUP)
    el = jnp.where(in_grp, logits, -jnp.inf)
    v1 = jnp.max(el, axis=-1, keepdims=True)
    i1 = jnp.min(jnp.where(el == v1, lane, big), axis=-1, keepdims=True)
    el2 = jnp.where(lane == i1, -jnp.inf, el)
    v2 = jnp.max(el2, axis=-1, keepdims=True)
    i2 = jnp.min(jnp.where(el2 == v2, lane, big), axis=-1, keepdims=True)
    t = jnp.exp(v2 - v1)
    w1 = p_group / (1.0 + t)
    w2 = p_group * t / (1.0 + t)

    oh1 = lane == i1
    oh2 = lane == i2
    cnt = jnp.where(oh1 | oh2, 1.0, 0.0)
    tri_n = tri_ref.shape[0]
    cnt_bf = cnt.astype(BF16)
    run = base_ref[0:1, :]
    parts = []
    for sb in range(tm // tri_n):
        rows = slice(sb * tri_n, (sb + 1) * tri_n)
        parts.append(_dot(tri_ref[...], cnt_bf[rows]) + run)
        run = run + jnp.sum(cnt[rows], axis=0, keepdims=True)
    prefix = jnp.concatenate(parts, axis=0)
    r1 = jnp.sum(jnp.where(oh1, prefix, 0.0), axis=-1, keepdims=True)
    r2 = jnp.sum(jnp.where(oh2, prefix, 0.0), axis=-1, keepdims=True)
    new_base = base_ref[...] + jnp.sum(cnt, axis=0, keepdims=True)
    base_ref[...] = new_base
    cnt_out[...] = new_base

    route = jnp.where(lane == 0.0, i1, jnp.where(lane == 1.0, i2, jnp.where(
        lane == 2.0, r1, jnp.where(lane == 3.0, r2, jnp.where(
            lane == 4.0, w1, jnp.where(lane == 5.0, w2, 0.0))))))
    route_out[0] = route
    rt_out[...] = route.T[0:SUBLANES, :].astype(jnp.int32)


def _outproj(x, attn, conv, mod, ga, wout, g2, wr, br):
    bsz, seq, d = x.shape
    tm = TM_OUT
    aw = attn.shape[-1]
    cwd = conv.shape[-1]
    n_s = seq // tm
    rpt = d // LANES
    tri_n = min(tm, TRI)
    tri = (lax.broadcasted_iota(jnp.int32, (tri_n, tri_n), 1)
           < lax.broadcasted_iota(jnp.int32, (tri_n, tri_n), 0)).astype(BF16)
    const2 = lambda b, s: (0, 0)
    tile3 = lambda b, s: (b, s, 0)
    return pl.pallas_call(
        _outproj_kernel,
        out_shape=(
            jax.ShapeDtypeStruct((bsz, seq, d), F32),
            jax.ShapeDtypeStruct((bsz * seq * rpt, LANES), F32),
            jax.ShapeDtypeStruct((bsz, seq, LANES), F32),
            jax.ShapeDtypeStruct((SUBLANES, bsz * seq), jnp.int32),
            jax.ShapeDtypeStruct((SUBLANES, LANES), F32),
        ),
        grid=(bsz, n_s),
        in_specs=[
            pl.BlockSpec((1, tm, d), tile3),
            pl.BlockSpec((1, tm, aw), tile3),
            pl.BlockSpec((1, tm, cwd), tile3),
            pl.BlockSpec((1, N_MOD, d), lambda b, s: (b, 0, 0)),
            pl.BlockSpec((1, aw), const2),
            pl.BlockSpec(wout.shape, const2),
            pl.BlockSpec((1, d), const2),
            pl.BlockSpec(wr.shape, const2),
            pl.BlockSpec((1, LANES), const2),
            pl.BlockSpec((tri_n, tri_n), const2),
        ],
        out_specs=(
            pl.BlockSpec((1, tm, d), tile3),
            pl.BlockSpec((tm * rpt, LANES), lambda b, s: (b * n_s + s, 0)),
            pl.BlockSpec((1, tm, LANES), tile3),
            pl.BlockSpec((SUBLANES, tm), lambda b, s: (0, b * n_s + s)),
            pl.BlockSpec((SUBLANES, LANES), const2),
        ),
        scratch_shapes=[pltpu.VMEM((SUBLANES, LANES), F32)],
        compiler_params=pltpu.CompilerParams(
            dimension_semantics=("arbitrary", "arbitrary"), vmem_limit_bytes=VMEM_LIMIT),
        name="outproj",
    )(x, attn, conv, mod, ga, wout, g2, wr, br, tri)


def _dest_kernel(starts_ref, rt_ref, dest_out):
    rt = rt_ref[...]
    ex = rt[0:TOP_K, :]
    start = jnp.zeros(ex.shape, jnp.int32)
    for e in range(N_EXPERTS):
        start = jnp.where(ex == e, starts_ref[e], start)
    dest = start + rt[TOP_K:2 * TOP_K, :]
    dest_out[...] = jnp.concatenate(
        [dest, jnp.zeros((SUBLANES - TOP_K, dest.shape[1]), jnp.int32)], axis=0)


def _dest(starts, rt):
    t = rt.shape[1]
    tm = TM_DEST
    grid_spec = pltpu.PrefetchScalarGridSpec(
        num_scalar_prefetch=1,
        grid=(t // tm,),
        in_specs=[pl.BlockSpec((SUBLANES, tm), lambda i, st: (0, i))],
        out_specs=pl.BlockSpec((SUBLANES, tm), lambda i, st: (0, i)),
    )
    return pl.pallas_call(
        _dest_kernel,
        out_shape=jax.ShapeDtypeStruct((SUBLANES, t), jnp.int32),
        grid_spec=grid_spec,
        compiler_params=pltpu.CompilerParams(
            dimension_semantics=("arbitrary",), vmem_limit_bytes=VMEM_LIMIT),
        name="dest",
    )(starts, rt)


def _token_copy(src, src_tok, dst, dst_tok, sem, rpt):
    return pltpu.make_async_copy(src.at[pl.ds(src_tok * rpt, rpt)], dst.at[pl.ds(dst_tok * rpt, rpt)],
                                 sem)


def _dispatch_kernel(dest_ref, h_hbm, xs_hbm, hbuf, sem_in, sem_out, *, tm, rpt):
    i = pl.program_id(0)
    n = pl.num_programs(0)
    slot = i % DISP_SLOTS

    def tile_load(tile, slot_idx):
        return pltpu.make_async_copy(h_hbm.at[pl.ds(tile * tm * rpt, tm * rpt)], hbuf.at[slot_idx],
                                     sem_in.at[slot_idx])

    def drain(slot_idx):
        def body(t, carry):
            for k in range(TOP_K):
                _token_copy(hbuf.at[slot_idx], 0, xs_hbm, 0, sem_out.at[slot_idx], rpt).wait()
            return carry
        lax.fori_loop(0, tm, body, 0, unroll=8)

    @pl.when(i == 0)
    def _():
        tile_load(0, 0).start()

    @pl.when((i == 0) & (n > 1))
    def _():
        tile_load(1, 1).start()

    tile_load(i, slot).wait()
    for t in range(tm):
        for k in range(TOP_K):
            _token_copy(hbuf.at[slot], t, xs_hbm, dest_ref[k, t], sem_out.at[slot], rpt).start(
                priority=k)

    refill = (i + 2) % DISP_SLOTS

    @pl.when(i > 0)
    def _():
        drain(refill)

    @pl.when(i + 2 < n)
    def _():
        tile_load(i + 2, refill).start()

    @pl.when(i == n - 1)
    def _():
        drain(slot)


def _dispatch(dest, h, rpt):
    t = h.shape[0] // rpt
    tm = TM_DISP
    return pl.pallas_call(
        functools.partial(_dispatch_kernel, tm=tm, rpt=rpt),
        out_shape=jax.ShapeDtypeStruct((t * TOP_K * rpt, LANES), h.dtype),
        grid=(t // tm,),
        in_specs=[
            pl.BlockSpec((SUBLANES, tm), lambda i: (0, i), memory_space=pltpu.SMEM),
            pl.BlockSpec(memory_space=pl.ANY),
        ],
        out_specs=pl.BlockSpec(memory_space=pl.ANY),
        scratch_shapes=[pltpu.VMEM((DISP_SLOTS, tm * rpt, LANES), h.dtype),
                        pltpu.SemaphoreType.DMA((DISP_SLOTS,)),
                        pltpu.SemaphoreType.DMA((DISP_SLOTS,))],
        compiler_params=pltpu.CompilerParams(
            dimension_semantics=("arbitrary",), vmem_limit_bytes=VMEM_LIMIT),
        name="dispatch",
    )(dest, h)


def _experts_kernel(blk_ref, exp_ref, lo_ref, hi_ref, x_ref, wg_ref, wu_ref, wd_ref, o_ref, acc_ref,
                    wgu_bf, wd_bf, *, n_items, rpt):
    i = pl.program_id(0)
    lo = lo_ref[i]
    hi = hi_ref[i]
    blk = blk_ref[i]
    first = (i == 0) | (blk != blk_ref[jnp.maximum(i - 1, 0)])

    @pl.when(i == 0)
    def _():
        acc_ref[...] = jnp.zeros(acc_ref.shape, F32)

    @pl.when((i == 0) | (exp_ref[i] != exp_ref[jnp.maximum(i - 1, 0)]))
    def _():
        de = wd_ref.shape[1]
        wgu_bf[:, :de] = wg_ref[0].astype(BF16)
        wgu_bf[:, de:] = wu_ref[0].astype(BF16)
        wd_bf[...] = wd_ref[0].astype(BF16)

    nxt = blk_ref[jnp.minimum(i + 1, n_items - 1)]
    last = (i == n_items - 1) | (blk != nxt)
    whole = first & last

    def body(masked):
        for sb in range(BLK // SUB):
            rows = pl.ds(sb * SUB, SUB)
            tok_rows = pl.ds(sb * SUB * rpt, SUB * rpt)
            x = _load_token_rows(x_ref.at[tok_rows], SUB, rpt).astype(BF16)
            gu = _dot(x, wgu_bf[...])
            de = gu.shape[1] // 2
            g = gu[:, :de]
            a = (g * jax.nn.sigmoid(g) * gu[:, de:]).astype(BF16)
            y = _dot(a, wd_bf[...])
            if masked:
                row = lax.broadcasted_iota(jnp.int32, (SUB, 1), 0) + sb * SUB
                keep = (row >= lo) & (row < hi)
                y = jnp.where(first, 0.0, acc_ref[rows, :]) + jnp.where(keep, y, 0.0)
                acc_ref[rows, :] = y
            _store_token_rows(o_ref.at[tok_rows], y)

    pl.when((hi > lo) & whole)(functools.partial(body, False))
    pl.when((hi > lo) & jnp.logical_not(whole))(functools.partial(body, True))


def _experts(item_blk, item_exp, item_lo, item_hi, xs, wg, wu, wd, rpt):
    a = xs.shape[0] // rpt
    d = rpt * LANES
    de = wd.shape[1]
    n_items = item_blk.shape[0]
    grid_spec = pltpu.PrefetchScalarGridSpec(
        num_scalar_prefetch=4,
        grid=(n_items,),
        in_specs=[
            pl.BlockSpec((BLK * rpt, LANES), lambda i, blk, ex, lo, hi: (blk[i], 0)),
            pl.BlockSpec((1, d, de), lambda i, blk, ex, lo, hi: (ex[i], 0, 0)),
            pl.BlockSpec((1, d, de), lambda i, blk, ex, lo, hi: (ex[i], 0, 0)),
            pl.BlockSpec((1, de, d), lambda i, blk, ex, lo, hi: (ex[i], 0, 0)),
        ],
        out_specs=pl.BlockSpec((BLK * rpt, LANES), lambda i, blk, ex, lo, hi: (blk[i], 0)),
        scratch_shapes=[pltpu.VMEM((BLK, d), F32), pltpu.VMEM((d, 2 * de), BF16),
                        pltpu.VMEM((de, d), BF16)],
    )
    return pl.pallas_call(
        functools.partial(_experts_kernel, n_items=n_items, rpt=rpt),
        out_shape=jax.ShapeDtypeStruct((a * rpt, LANES), F32),
        grid_spec=grid_spec,
        compiler_params=pltpu.CompilerParams(
            dimension_semantics=("arbitrary",), vmem_limit_bytes=VMEM_LIMIT),
        name="experts",
    )(item_blk, item_exp, item_lo, item_hi, xs, wg, wu, wd)


def _work_items(counts, n_rows):
    n_blocks = n_rows // BLK
    n_items = n_blocks + N_EXPERTS - 1
    counts = counts.astype(jnp.int32)
    ends = jnp.cumsum(counts)
    starts = ends - counts
    first_blk = starts // BLK
    last_blk = jnp.maximum(ends - 1, 0) // BLK
    n_e = jnp.where(counts > 0, last_blk - first_blk + 1, 0)
    item_end = jnp.cumsum(n_e)
    item_start = item_end - n_e
    idx = jnp.arange(n_items, dtype=jnp.int32)
    total = item_end[-1]
    ex = jnp.minimum(jnp.sum((idx[:, None] >= item_end[None, :]).astype(jnp.int32), axis=1),
                     N_EXPERTS - 1)
    onehot = (ex[:, None] == jnp.arange(N_EXPERTS, dtype=jnp.int32)[None, :]).astype(jnp.int32)
    pick = lambda table: jnp.sum(onehot * table[None, :], axis=1)
    blk = pick(first_blk) + (idx - pick(item_start))
    lo = jnp.maximum(pick(starts), blk * BLK) - blk * BLK
    hi = jnp.minimum(pick(ends), (blk + 1) * BLK) - blk * BLK
    valid = idx < total
    last_ex = jnp.max(jnp.where(valid, ex, 0))
    blk = jnp.where(valid, blk, n_blocks - 1).astype(jnp.int32)
    ex = jnp.where(valid, ex, last_ex).astype(jnp.int32)
    lo = jnp.where(valid, lo, 0).astype(jnp.int32)
    hi = jnp.where(valid, hi, 0).astype(jnp.int32)
    return blk, ex, lo, hi


def _combine_kernel(dest_ref, dest_one_ref, dest_ahead_ref, ys_hbm, x1_ref, route_ref, mod_ref,
                    gf_ref, o_ref, ybuf, sems, *, tm, final_norm, rpt):
    i = pl.program_id(0)
    slot = i % N_SLOTS

    def gather(d_ref, slot_idx):
        for t in range(tm):
            for k in range(TOP_K):
                _token_copy(ys_hbm, d_ref[k, t], ybuf.at[slot_idx, k], t, sems.at[slot_idx],
                            rpt).start(priority=k)

    def drain(slot_idx):
        def body(t, carry):
            for k in range(TOP_K):
                _token_copy(ys_hbm, 0, ybuf.at[slot_idx, k], 0, sems.at[slot_idx], rpt).wait()
            return carry
        lax.fori_loop(0, tm, body, 0, unroll=8)

    @pl.when(i == 0)
    def _():
        gather(dest_ref, 0)
        gather(dest_one_ref, 1)

    drain(slot)
    r = route_ref[...]
    y = (r[:, 4:5] * _load_token_rows(ybuf.at[slot, 0], tm, rpt)
         + r[:, 5:6] * _load_token_rows(ybuf.at[slot, 1], tm, rpt))
    mod = mod_ref[0]
    x2 = x1_ref[...] + mod[5:6] * y
    o_ref[...] = _rms(x2, gf_ref[...]) if final_norm else x2

    gather(dest_ahead_ref, (i + GATHER_AHEAD) % N_SLOTS)

    @pl.when(i == pl.num_programs(0) - 1)
    def _():
        for extra in range(1, N_SLOTS):
            drain((i + extra) % N_SLOTS)


def _combine(dest, ys, x1, route, mod, gf, seq, final_norm):
    t, d = x1.shape
    tm = TM_COMB
    n_steps = t // tm
    per_b = seq // tm
    rpt = d // LANES
    return pl.pallas_call(
        functools.partial(_combine_kernel, tm=tm, final_norm=final_norm, rpt=rpt),
        out_shape=jax.ShapeDtypeStruct((t, d), F32),
        grid=(n_steps,),
        in_specs=[
            pl.BlockSpec((SUBLANES, tm), lambda i: (0, i), memory_space=pltpu.SMEM),
            pl.BlockSpec((SUBLANES, tm), lambda i: (0, min(1, n_steps - 1)),
                         memory_space=pltpu.SMEM),
            pl.BlockSpec((SUBLANES, tm), lambda i: (0, jnp.minimum(i + GATHER_AHEAD, n_steps - 1)),
                         memory_space=pltpu.SMEM),
            pl.BlockSpec(memory_space=pl.ANY),
            pl.BlockSpec((tm, d), lambda i: (i, 0)),
            pl.BlockSpec((tm, LANES), lambda i: (i, 0)),
            pl.BlockSpec((1, N_MOD, d), lambda i: (i // per_b, 0, 0)),
            pl.BlockSpec((1, d), lambda i: (0, 0)),
        ],
        out_specs=pl.BlockSpec((tm, d), lambda i: (i, 0)),
        scratch_shapes=[pltpu.VMEM((N_SLOTS, TOP_K, tm * rpt, LANES), F32),
                        pltpu.SemaphoreType.DMA((N_SLOTS,))],
        compiler_params=pltpu.CompilerParams(
            dimension_semantics=("arbitrary",), vmem_limit_bytes=VMEM_LIMIT),
        name="combine",
    )(dest, dest, dest, ys, x1, route, mod, gf)


def _prep_in_weights(w_in, q_rank, kv_rank, conv_dim):
    o1 = q_rank
    o2 = o1 + kv_rank
    o3 = o2 + D_ROPE
    d = w_in.shape[0]
    w_kr = w_in[:, o2:o3]
    half = D_ROPE // 2
    w_kr_rot = jnp.concatenate([-w_kr[:, half:], w_kr[:, :half]], axis=1)
    zl = jnp.zeros((d, D_NOPE), w_in.dtype)
    zr = jnp.zeros((d, LANES - D_NOPE - D_ROPE), w_in.dtype)
    return jnp.concatenate(
        [w_in[:, :o2], zl, w_kr, zr, zl, w_kr_rot, zr, w_in[:, o3:]], axis=1).astype(BF16)


def _prep_q_weights(w_uq):
    r = w_uq.shape[0]
    w = w_uq.reshape(r, HEADS, D_NOPE + D_ROPE)
    nope, rope = w[..., :D_NOPE], w[..., D_NOPE:]
    half = D_ROPE // 2
    rot = jnp.concatenate([-rope[..., half:], rope[..., :half]], axis=-1)
    zr = jnp.zeros((r, HEADS, HEAD_PAD - D_NOPE - D_ROPE), w_uq.dtype)
    w1 = jnp.concatenate([nope, rope, zr], axis=-1).reshape(r, HEADS * HEAD_PAD)
    w2 = jnp.concatenate([jnp.zeros_like(nope), rot, zr], axis=-1).reshape(r, HEADS * HEAD_PAD)
    return w1.astype(BF16), w2.astype(BF16)


def _prep_k_weights(w_uk):
    r = w_uk.shape[0]
    w = w_uk.reshape(r, HEADS, D_NOPE)
    zr = jnp.zeros((r, HEADS, HEAD_PAD - D_NOPE), w_uk.dtype)
    return jnp.concatenate([w, zr], axis=-1).reshape(r, HEADS * HEAD_PAD).astype(BF16)


def _rope_freqs():
    inv_freq = 1.0 / (ROPE_THETA ** (jnp.arange(0, D_ROPE, 2, dtype=F32) / D_ROPE))
    return inv_freq.reshape(D_ROPE // 2, 1)


def kernel(x, c, positions, w_ada, b_ada, norm1_g, w_in, q_norm_g, kv_norm_g, w_uq, w_uk, w_uv,
           conv_w, conv_b, attn_out_g, conv_out_g, w_out, norm2_g, w_router_group, b_router_group,
           w_router_expert, b_router_expert, w_exp_gate, w_exp_up, w_exp_down, final_g):
    bsz, seq, d = x.shape
    depth = w_ada.shape[0]
    q_rank = q_norm_g.shape[-1]
    kv_rank = kv_norm_g.shape[-1]
    conv_dim = conv_w.shape[-1]
    t = bsz * seq
    assert seq % max(TM_PROJ, TQ, TM_OUT, TM_COMB) == 0 and TQ % TK == 0 and TM_PROJ % TK == 0
    assert CHUNK & (CHUNK - 1) == 0 and TQ % CHUNK == 0
    assert t % max(TM_DEST, TM_DISP, TM_COMB) == 0 and (t * TOP_K) % BLK == 0
    assert w_uq.shape[-1] == HEADS * (D_NOPE + D_ROPE) and w_uv.shape[-1] == HEADS * D_V

    pos = positions.reshape(bsz, 1, seq)
    invf = _rope_freqs()
    row = lambda v: v.reshape(1, -1)
    for l in range(depth):
        mod = _adaln(c, w_ada[l], b_ada[l]).reshape(bsz, N_MOD, d)
        win = _prep_in_weights(w_in[l], q_rank, kv_rank, conv_dim)
        wq1, wq2 = _prep_q_weights(w_uq[l])
        wk = _prep_k_weights(w_uk[l])
        wvt = w_uv[l].T.astype(BF16)
        q, k, vt, conv = _inproj(x, mod, pos, row(norm1_g[l]), win, row(q_norm_g[l]),
                                 row(kv_norm_g[l]), wq1, wq2, wk, wvt, invf, conv_w[l],
                                 row(conv_b[l]), row(conv_out_g[l]))
        attn = _attention(q, k, vt)

        wr = jnp.concatenate(
            [w_router_expert[l], w_router_group[l],
             jnp.zeros((d, LANES - N_EXPERTS - N_GROUPS), F32)], axis=1).astype(BF16)
        br = jnp.concatenate(
            [b_router_expert[l], b_router_group[l],
             jnp.zeros((LANES - N_EXPERTS - N_GROUPS,), F32)]).reshape(1, LANES)
        x1, h2, route, rt, cnt = _outproj(x, attn, conv, mod, row(attn_out_g[l]),
                                      w_out[l].astype(BF16), row(norm2_g[l]), wr, br)
        route = route.reshape(t, LANES)
        counts = cnt[0, :N_EXPERTS].astype(jnp.int32)
        starts = jnp.cumsum(counts) - counts
        rpt = d // LANES
        dest = _dest(starts, rt)
        xs = _dispatch(dest, h2, rpt)
        items = _work_items(counts, t * TOP_K)
        ys = _experts(*items, xs, w_exp_gate[l], w_exp_up[l], w_exp_down[l], rpt)
        out = _combine(dest, ys, x1.reshape(t, d), route, mod, row(final_g), seq, l == depth - 1)
        x = out.reshape(bsz, seq, d)
    return x
```

```python
import functools
import math

import jax
import jax.numpy as jnp
from jax import lax
from jax.experimental import pallas as pl
from jax.experimental.pallas import tpu as pltpu

F32 = jnp.float32
BF16 = jnp.bfloat16

CHUNK = 64
HEADS = 8
D_NOPE = 64
D_ROPE = 32
D_V = 64
ROPE_THETA = 10000.0
CONV_WIDTH = 3
N_GROUPS = 4
EXPERTS_PER_GROUP = 8
N_EXPERTS = N_GROUPS * EXPERTS_PER_GROUP
TOP_K = 2
N_MOD = 6
RMS_EPS = 1e-6
NEG_INF = -1e30
LOG2_E = math.log2(math.e)

LANES = 128
SUBLANES = 8
HEAD_PAD = LANES
DEN_ROWS = 16
VMEM_LIMIT = 56 * 1024 * 1024

TM_PROJ = 1024
TQ = 512
TK = 256
TM_OUT = 1024
TM_DEST = 2048
TM_DISP = 1024
BLK = 512
SUB = 256
TM_COMB = 256
TN_ADA = 512
TRI = 256
GATHER_AHEAD = 2
N_SLOTS = GATHER_AHEAD + 1
DISP_SLOTS = 3


def _rms(x, g):
    return x * lax.rsqrt(jnp.mean(x * x, axis=-1, keepdims=True) + RMS_EPS) * g


def _dot(a, b):
    return jnp.dot(a, b, preferred_element_type=F32)


def _dot_nt(a, b):
    return lax.dot_general(a, b, (((1,), (1,)), ((), ())), preferred_element_type=F32)


def _store_token_rows(ref, x):
    n, d = x.shape
    r = d // LANES
    for j in range(r):
        ref[pl.ds(j, n, stride=r), :] = x[:, j * LANES:(j + 1) * LANES]


def _load_token_rows(ref, n, r):
    return jnp.concatenate([ref[pl.ds(j, n, stride=r), :] for j in range(r)], axis=1)


def _split_bf16(x):
    hi = x.astype(BF16)
    lo = (x - hi.astype(F32)).astype(BF16)
    return hi, lo


def _adaln_kernel(c_ref, w_ref, b_ref, o_ref):
    c = c_ref[...]
    ca = c * jax.nn.sigmoid(c)
    c_hi, c_lo = _split_bf16(ca)
    w_hi, w_lo = _split_bf16(w_ref[...])
    acc = _dot(c_hi, w_hi) + _dot(c_hi, w_lo) + _dot(c_lo, w_hi)
    o_ref[...] = acc + b_ref[...]


def _adaln(c, w, b):
    bsz, d = c.shape
    n = w.shape[1]
    return pl.pallas_call(
        _adaln_kernel,
        out_shape=jax.ShapeDtypeStruct((bsz, n), F32),
        grid=(n // TN_ADA,),
        in_specs=[
            pl.BlockSpec((bsz, d), lambda j: (0, 0)),
            pl.BlockSpec((d, TN_ADA), lambda j: (0, j)),
            pl.BlockSpec((1, TN_ADA), lambda j: (0, j)),
        ],
        out_specs=pl.BlockSpec((bsz, TN_ADA), lambda j: (0, j)),
        compiler_params=pltpu.CompilerParams(
            dimension_semantics=("arbitrary",), vmem_limit_bytes=VMEM_LIMIT),
        name="adaln",
    )(c, w, b.reshape(1, n))


def _inproj_kernel(x_ref, mod_ref, pos_ref, g1_ref, win_ref, gq_ref, gkv_ref, wq1_ref, wq2_ref,
                   wk_ref, wvt_ref, invf_ref, cw_ref, cb_ref, gc_ref,
                   q_out, k_out, vt_out, conv_out, zbuf, *, tm, q_rank, kv_rank, conv_dim):
    si = pl.program_id(1)
    x = x_ref[0]
    mod = mod_ref[0]
    h = (_rms(x, g1_ref[...]) * (1.0 + mod[1:2]) + mod[0:1]).astype(BF16)
    proj = _dot(h, win_ref[...])

    o_kv = q_rank
    o_kr = o_kv + kv_rank
    o_u = o_kr + 2 * LANES
    q_c = proj[:, 0:q_rank]
    kv_c = proj[:, o_kv:o_kv + kv_rank]
    kr = proj[:, o_kr:o_kr + LANES]
    kr_rot = proj[:, o_kr + LANES:o_kr + 2 * LANES]
    u = proj[:, o_u:o_u + conv_dim]
    b_gate = proj[:, o_u + conv_dim:o_u + 2 * conv_dim]
    c_gate = proj[:, o_u + 2 * conv_dim:o_u + 3 * conv_dim]

    ang_t = invf_ref[...] * pos_ref[0].astype(F32)

    def head_lanes(tab_t):
        zl = jnp.zeros((D_NOPE, tm), F32)
        zr = jnp.zeros((LANES - D_NOPE - D_ROPE, tm), F32)
        return jnp.concatenate([zl, tab_t, tab_t, zr], axis=0).T

    cos_t = head_lanes(jnp.cos(ang_t))
    sin_t = head_lanes(jnp.sin(ang_t))
    lane = lax.broadcasted_iota(jnp.int32, (1, LANES), 1)
    q_scale = LOG2_E / math.sqrt(D_NOPE + D_ROPE)
    q_cos = jnp.where(lane < D_NOPE, q_scale, cos_t * q_scale)
    q_sin = sin_t * q_scale

    qn = _rms(q_c, gq_ref[...]).astype(BF16)
    qa = _dot(qn, wq1_ref[...])
    qb = _dot(qn, wq2_ref[...])
    kvn = _rms(kv_c, gkv_ref[...]).astype(BF16)
    kn = _dot(kvn, wk_ref[...])
    k_rope = kr * cos_t + kr_rot * sin_t
    for hd in range(HEADS):
        sl = slice(hd * HEAD_PAD, (hd + 1) * HEAD_PAD)
        q_out[0, hd] = (qa[:, sl] * q_cos + qb[:, sl] * q_sin).astype(BF16)
        k_out[0, hd] = (kn[:, sl] + k_rope).astype(BF16)

    vt = _dot_nt(wvt_ref[...], kvn)
    for j in range(tm // TK):
        vt_out[0, j] = vt[:, j * TK:(j + 1) * TK].astype(BF16)

    z = c_gate * u

    @pl.when(si == 0)
    def _():
        zbuf[0:SUBLANES, :] = jnp.zeros((SUBLANES, conv_dim), F32)

    zbuf[SUBLANES:SUBLANES + tm, :] = z
    z1 = zbuf[SUBLANES - 1:SUBLANES - 1 + tm, :]
    z2 = zbuf[SUBLANES - 2:SUBLANES - 2 + tm, :]
    cw = cw_ref[...]
    zc = cw[0:1] * z2 + cw[1:2] * z1 + cw[2:3] * z + cb_ref[...]
    conv = b_gate * zc
    conv_out[0] = _rms(conv, gc_ref[...]).astype(BF16)
    zbuf[0:SUBLANES, :] = zbuf[tm:tm + SUBLANES, :]


def _inproj(x, mod, pos, g1, win, gq, gkv, wq1, wq2, wk, wvt, invf, cw, cb, gc):
    bsz, seq, d = x.shape
    tm = TM_PROJ
    q_rank, kv_rank = wq1.shape[0], wk.shape[0]
    conv_dim = cw.shape[1]
    hv = wvt.shape[0]
    n_s = seq // tm
    const2 = lambda b, s: (0, 0)
    kern = functools.partial(_inproj_kernel, tm=tm, q_rank=q_rank, kv_rank=kv_rank,
                             conv_dim=conv_dim)
    return pl.pallas_call(
        kern,
        out_shape=(
            jax.ShapeDtypeStruct((bsz, HEADS, seq, HEAD_PAD), BF16),
            jax.ShapeDtypeStruct((bsz, HEADS, seq, HEAD_PAD), BF16),
            jax.ShapeDtypeStruct((bsz, seq // TK, hv, TK), BF16),
            jax.ShapeDtypeStruct((bsz, seq, conv_dim), BF16),
        ),
        grid=(bsz, n_s),
        in_specs=[
            pl.BlockSpec((1, tm, d), lambda b, s: (b, s, 0)),
            pl.BlockSpec((1, N_MOD, d), lambda b, s: (b, 0, 0)),
            pl.BlockSpec((1, 1, tm), lambda b, s: (b, 0, s)),
            pl.BlockSpec((1, d), const2),
            pl.BlockSpec(win.shape, const2),
            pl.BlockSpec((1, q_rank), const2),
            pl.BlockSpec((1, kv_rank), const2),
            pl.BlockSpec(wq1.shape, const2),
            pl.BlockSpec(wq2.shape, const2),
            pl.BlockSpec(wk.shape, const2),
            pl.BlockSpec(wvt.shape, const2),
            pl.BlockSpec(invf.shape, const2),
            pl.BlockSpec(cw.shape, const2),
            pl.BlockSpec((1, conv_dim), const2),
            pl.BlockSpec((1, conv_dim), const2),
        ],
        out_specs=(
            pl.BlockSpec((1, HEADS, tm, HEAD_PAD), lambda b, s: (b, 0, s, 0)),
            pl.BlockSpec((1, HEADS, tm, HEAD_PAD), lambda b, s: (b, 0, s, 0)),
            pl.BlockSpec((1, tm // TK, hv, TK), lambda b, s: (b, s, 0, 0)),
            pl.BlockSpec((1, tm, conv_dim), lambda b, s: (b, s, 0)),
        ),
        scratch_shapes=[pltpu.VMEM((tm + 2 * SUBLANES, conv_dim), F32)],
        compiler_params=pltpu.CompilerParams(
            dimension_semantics=("arbitrary", "arbitrary"), vmem_limit_bytes=VMEM_LIMIT),
        name="inproj",
    )(x, mod, pos, g1, win, gq, gkv, wq1, wq2, wk, wvt, invf, cw, cb, gc)


def _attn_kernel(q_ref, k_ref, vt_ref, o_ref, *, n_q):
    row = lax.broadcasted_iota(jnp.int32, (TK, TQ), 0)
    col = lax.broadcasted_iota(jnp.int32, (TK, TQ), 1)
    shift = CHUNK.bit_length() - 1
    k_per_q = TQ // TK
    ones_rows = jnp.ones((DEN_ROWS, TK), BF16)

    def scores(hh, qi, j):
        return _dot_nt(k_ref[0, hh, j * TK:(j + 1) * TK, :], q_ref[0, hh, qi * TQ:(qi + 1) * TQ, :])

    def softmax_pv(hh, qi, j, s_t, state):
        m, acc = state
        off = j - qi * k_per_q
        if off >= 0:
            ok = ((row + off * TK) >> shift) <= (col >> shift)
            s_t = jnp.where(ok, s_t, NEG_INF)
        m_new = jnp.maximum(m, jnp.max(s_t, axis=0, keepdims=True))
        p = jnp.exp2(s_t - m_new).astype(BF16)
        alpha = jnp.exp2(m - m_new)
        v_aug = jnp.concatenate([vt_ref[0, j, hh * D_V:(hh + 1) * D_V, :], ones_rows], axis=0)
        return m_new, alpha * acc + _dot(v_aug, p)

    steps = [(qi, j) for qi in range(n_q) for j in range((qi + 1) * k_per_q)]
    init = (jnp.full((1, TQ), NEG_INF, F32), jnp.zeros((D_V + DEN_ROWS, TQ), F32))
    s_cur = tuple(scores(hh, *steps[0]) for hh in range(2))
    states = None
    for idx, (qi, j) in enumerate(steps):
        s_next = None
        if idx + 1 < len(steps):
            s_next = tuple(scores(hh, *steps[idx + 1]) for hh in range(2))
        if j == 0:
            states = (init, init)
        states = tuple(softmax_pv(hh, qi, j, s_cur[hh], states[hh]) for hh in range(2))
        if j == (qi + 1) * k_per_q - 1:
            o_t = jnp.concatenate([acc[:D_V] / acc[D_V:D_V + 1] for (_, acc) in states],
                                  axis=0)
            o_ref[0, qi * TQ:(qi + 1) * TQ, :] = o_t.T.astype(BF16)
        s_cur = s_next


def _attention(q, k, vt):
    bsz, _, seq, _ = q.shape
    n_pair = HEADS // 2
    return pl.pallas_call(
        functools.partial(_attn_kernel, n_q=seq // TQ),
        out_shape=jax.ShapeDtypeStruct((bsz, seq, HEADS * D_V), BF16),
        grid=(bsz, n_pair),
        in_specs=[
            pl.BlockSpec((1, 2, seq, HEAD_PAD), lambda b, p: (b, p, 0, 0)),
            pl.BlockSpec((1, 2, seq, HEAD_PAD), lambda b, p: (b, p, 0, 0)),
            pl.BlockSpec((1, seq // TK, 2 * D_V, TK), lambda b, p: (b, 0, p, 0)),
        ],
        out_specs=pl.BlockSpec((1, seq, 2 * D_V), lambda b, p: (b, 0, p)),
        compiler_params=pltpu.CompilerParams(
            dimension_semantics=("arbitrary", "arbitrary"), vmem_limit_bytes=VMEM_LIMIT),
        name="attn",
    )(q, k, vt)


def _outproj_kernel(x_ref, attn_ref, conv_ref, mod_ref, ga_ref, wout_ref, g2_ref, wr_ref, br_ref,
                    tri_ref, x1_out, h2_out, route_out, rt_out, cnt_out, base_ref):
    first = (pl.program_id(0) == 0) & (pl.program_id(1) == 0)

    @pl.when(first)
    def _():
        base_ref[...] = jnp.zeros(base_ref.shape, F32)

    mod = mod_ref[0]
    an = _rms(attn_ref[0].astype(F32), ga_ref[...]).astype(BF16)
    mixed = jnp.concatenate([an, conv_ref[0]], axis=-1)
    y = _dot(mixed, wout_ref[...])
    x1 = x_ref[0] + mod[2:3] * y
    x1_out[0] = x1
    h2 = _rms(x1, g2_ref[...]) * (1.0 + mod[4:5]) + mod[3:4]
    _store_token_rows(h2_out, h2)
    logits = _dot(h2.astype(BF16), wr_ref[...]) + br_ref[...]

    tm = logits.shape[0]
    lane = lax.broadcasted_iota(jnp.int32, (tm, LANES), 1).astype(F32)
    big = float(LANES)
    is_grp = (lane >= N_EXPERTS) & (lane < N_EXPERTS + N_GROUPS)
    gl = jnp.where(is_grp, logits, -jnp.inf)
    gmax = jnp.max(gl, axis=-1, keepdims=True)
    gidx = jnp.min(jnp.where(gl == gmax, lane, big), axis=-1, keepdims=True) - N_EXPERTS
    p_group = 1.0 / jnp.sum(jnp.where(is_grp, jnp.exp(logits - gmax), 0.0), axis=-1, keepdims=True)
    lo = gidx * EXPERTS_PER_GROUP
    in_grp = (lane >= lo) & (lane < lo + EXPERTS_PER_GROUP)
    el = jnp.where(in_grp, logits, -jnp.inf)
    v1 = jnp.max(el, axis=-1, keepdims=True)
    i1 = jnp.min(jnp.where(el == v1, lane, big), axis=-1, keepdims=True)
    el2 = jnp.where(lane == i1, -jnp.inf, el)
    v2 = jnp.max(el2, axis=-1, keepdims=True)
    i2 = jnp.min(jnp.where(el2 == v2, lane, big), axis=-1, keepdims=True)
    t = jnp.exp(v2 - v1)
    w1 = p_group / (1.0 + t)
    w2 = p_group * t / (1.0 + t)

    oh1 = lane == i1
    oh2 = lane == i2
    cnt = jnp.where(oh1 | oh2, 1.0, 0.0)
    tri_n = tri_ref.shape[0]
    cnt_bf = cnt.astype(BF16)
    run = base_ref[0:1, :]
    parts = []
    for sb in range(tm // tri_n):
        rows = slice(sb * tri_n, (sb + 1) * tri_n)
        parts.append(_dot(tri_ref[...], cnt_bf[rows]) + run)
        run = run + jnp.sum(cnt[rows], axis=0, keepdims=True)
    prefix = jnp.concatenate(parts, axis=0)
    r1 = jnp.sum(jnp.where(oh1, prefix, 0.0), axis=-1, keepdims=True)
    r2 = jnp.sum(jnp.where(oh2, prefix, 0.0), axis=-1, keepdims=True)
    new_base = base_ref[...] + jnp.sum(cnt, axis=0, keepdims=True)
    base_ref[...] = new_base
    cnt_out[...] = new_base

    route = jnp.where(lane == 0.0, i1, jnp.where(lane == 1.0, i2, jnp.where(
        lane == 2.0, r1, jnp.where(lane == 3.0, r2, jnp.where(
            lane == 4.0, w1, jnp.where(lane == 5.0, w2, 0.0))))))
    route_out[0] = route
    rt_out[...] = route.T[0:SUBLANES, :].astype(jnp.int32)


def _outproj(x, attn, conv, mod, ga, wout, g2, wr, br):
    bsz, seq, d = x.shape
    tm = TM_OUT
    aw = attn.shape[-1]
    cwd = conv.shape[-1]
    n_s = seq // tm
    rpt = d // LANES
    tri_n = min(tm, TRI)
    tri = (lax.broadcasted_iota(jnp.int32, (tri_n, tri_n), 1)
           < lax.broadcasted_iota(jnp.int32, (tri_n, tri_n), 0)).astype(BF16)
    const2 = lambda b, s: (0, 0)
    tile3 = lambda b, s: (b, s, 0)
    return pl.pallas_call(
        _outproj_kernel,
        out_shape=(
            jax.ShapeDtypeStruct((bsz, seq, d), F32),
            jax.ShapeDtypeStruct((bsz * seq * rpt, LANES), F32),
            jax.ShapeDtypeStruct((bsz, seq, LANES), F32),
            jax.ShapeDtypeStruct((SUBLANES, bsz * seq), jnp.int32),
            jax.ShapeDtypeStruct((SUBLANES, LANES), F32),
        ),
        grid=(bsz, n_s),
        in_specs=[
            pl.BlockSpec((1, tm, d), tile3),
            pl.BlockSpec((1, tm, aw), tile3),
            pl.BlockSpec((1, tm, cwd), tile3),
            pl.BlockSpec((1, N_MOD, d), lambda b, s: (b, 0, 0)),
            pl.BlockSpec((1, aw), const2),
            pl.BlockSpec(wout.shape, const2),
            pl.BlockSpec((1, d), const2),
            pl.BlockSpec(wr.shape, const2),
            pl.BlockSpec((1, LANES), const2),
            pl.BlockSpec((tri_n, tri_n), const2),
        ],
        out_specs=(
            pl.BlockSpec((1, tm, d), tile3),
            pl.BlockSpec((tm * rpt, LANES), lambda b, s: (b * n_s + s, 0)),
            pl.BlockSpec((1, tm, LANES), tile3),
            pl.BlockSpec((SUBLANES, tm), lambda b, s: (0, b * n_s + s)),
            pl.BlockSpec((SUBLANES, LANES), const2),
        ),
        scratch_shapes=[pltpu.VMEM((SUBLANES, LANES), F32)],
        compiler_params=pltpu.CompilerParams(
            dimension_semantics=("arbitrary", "arbitrary"), vmem_limit_bytes=VMEM_LIMIT),
        name="outproj",
    )(x, attn, conv, mod, ga, wout, g2, wr, br, tri)


def _dest_kernel(starts_ref, rt_ref, dest_out):
    rt = rt_ref[...]
    ex = rt[0:TOP_K, :]
    start = jnp.zeros(ex.shape, jnp.int32)
    for e in range(N_EXPERTS):
        start = jnp.where(ex == e, starts_ref[e], start)
    dest = start + rt[TOP_K:2 * TOP_K, :]
    dest_out[...] = jnp.concatenate(
        [dest, jnp.zeros((SUBLANES - TOP_K, dest.shape[1]), jnp.int32)], axis=0)


def _dest(starts, rt):
    t = rt.shape[1]
    tm = TM_DEST
    grid_spec = pltpu.PrefetchScalarGridSpec(
        num_scalar_prefetch=1,
        grid=(t // tm,),
        in_specs=[pl.BlockSpec((SUBLANES, tm), lambda i, st: (0, i))],
        out_specs=pl.BlockSpec((SUBLANES, tm), lambda i, st: (0, i)),
    )
    return pl.pallas_call(
        _dest_kernel,
        out_shape=jax.ShapeDtypeStruct((SUBLANES, t), jnp.int32),
        grid_spec=grid_spec,
        compiler_params=pltpu.CompilerParams(
            dimension_semantics=("arbitrary",), vmem_limit_bytes=VMEM_LIMIT),
        name="dest",
    )(starts, rt)


def _token_copy(src, src_tok, dst, dst_tok, sem, rpt):
    return pltpu.make_async_copy(src.at[pl.ds(src_tok * rpt, rpt)], dst.at[pl.ds(dst_tok * rpt, rpt)],
                                 sem)


def _dispatch_kernel(dest_ref, h_hbm, xs_hbm, hbuf, sem_in, sem_out, *, tm, rpt):
    i = pl.program_id(0)
    n = pl.num_programs(0)
    slot = i % DISP_SLOTS

    def tile_load(tile, slot_idx):
        return pltpu.make_async_copy(h_hbm.at[pl.ds(tile * tm * rpt, tm * rpt)], hbuf.at[slot_idx],
                                     sem_in.at[slot_idx])

    def drain(slot_idx):
        def body(t, carry):
            for k in range(TOP_K):
                _token_copy(hbuf.at[slot_idx], 0, xs_hbm, 0, sem_out.at[slot_idx], rpt).wait()
            return carry
        lax.fori_loop(0, tm, body, 0, unroll=8)

    @pl.when(i == 0)
    def _():
        tile_load(0, 0).start()

    @pl.when((i == 0) & (n > 1))
    def _():
        tile_load(1, 1).start()

    tile_load(i, slot).wait()
    for t in range(tm):
        for k in range(TOP_K):
            _token_copy(hbuf.at[slot], t, xs_hbm, dest_ref[k, t], sem_out.at[slot], rpt).start(
                priority=k)

    refill = (i + 2) % DISP_SLOTS

    @pl.when(i > 0)
    def _():
        drain(refill)

    @pl.when(i + 2 < n)
    def _():
        tile_load(i + 2, refill).start()

    @pl.when(i == n - 1)
    def _():
        drain(slot)


def _dispatch(dest, h, rpt):
    t = h.shape[0] // rpt
    tm = TM_DISP
    return pl.pallas_call(
        functools.partial(_dispatch_kernel, tm=tm, rpt=rpt),
        out_shape=jax.ShapeDtypeStruct((t * TOP_K * rpt, LANES), h.dtype),
        grid=(t // tm,),
        in_specs=[
            pl.BlockSpec((SUBLANES, tm), lambda i: (0, i), memory_space=pltpu.SMEM),
            pl.BlockSpec(memory_space=pl.ANY),
        ],
        out_specs=pl.BlockSpec(memory_space=pl.ANY),
        scratch_shapes=[pltpu.VMEM((DISP_SLOTS, tm * rpt, LANES), h.dtype),
                        pltpu.SemaphoreType.DMA((DISP_SLOTS,)),
                        pltpu.SemaphoreType.DMA((DISP_SLOTS,))],
        compiler_params=pltpu.CompilerParams(
            dimension_semantics=("arbitrary",), vmem_limit_bytes=VMEM_LIMIT),
        name="dispatch",
    )(dest, h)


def _experts_kernel(blk_ref, exp_ref, lo_ref, hi_ref, x_ref, wg_ref, wu_ref, wd_ref, o_ref, acc_ref,
                    wgu_bf, wd_bf, *, n_items, rpt):
    i = pl.program_id(0)
    lo = lo_ref[i]
    hi = hi_ref[i]
    blk = blk_ref[i]
    first = (i == 0) | (blk != blk_ref[jnp.maximum(i - 1, 0)])

    @pl.when(i == 0)
    def _():
        acc_ref[...] = jnp.zeros(acc_ref.shape, F32)

    @pl.when((i == 0) | (exp_ref[i] != exp_ref[jnp.maximum(i - 1, 0)]))
    def _():
        de = wd_ref.shape[1]
        wgu_bf[:, :de] = wg_ref[0].astype(BF16)
        wgu_bf[:, de:] = wu_ref[0].astype(BF16)
        wd_bf[...] = wd_ref[0].astype(BF16)

    nxt = blk_ref[jnp.minimum(i + 1, n_items - 1)]
    last = (i == n_items - 1) | (blk != nxt)
    whole = first & last

    def body(masked):
        for sb in range(BLK // SUB):
            rows = pl.ds(sb * SUB, SUB)
            tok_rows = pl.ds(sb * SUB * rpt, SUB * rpt)
            x = _load_token_rows(x_ref.at[tok_rows], SUB, rpt).astype(BF16)
            gu = _dot(x, wgu_bf[...])
            de = gu.shape[1] // 2
            g = gu[:, :de]
            a = (g * jax.nn.sigmoid(g) * gu[:, de:]).astype(BF16)
            y = _dot(a, wd_bf[...])
            if masked:
                row = lax.broadcasted_iota(jnp.int32, (SUB, 1), 0) + sb * SUB
                keep = (row >= lo) & (row < hi)
                y = jnp.where(first, 0.0, acc_ref[rows, :]) + jnp.where(keep, y, 0.0)
                acc_ref[rows, :] = y
            _store_token_rows(o_ref.at[tok_rows], y)

    pl.when((hi > lo) & whole)(functools.partial(body, False))
    pl.when((hi > lo) & jnp.logical_not(whole))(functools.partial(body, True))


def _experts(item_blk, item_exp, item_lo, item_hi, xs, wg, wu, wd, rpt):
    a = xs.shape[0] // rpt
    d = rpt * LANES
    de = wd.shape[1]
    n_items = item_blk.shape[0]
    grid_spec = pltpu.PrefetchScalarGridSpec(
        num_scalar_prefetch=4,
        grid=(n_items,),
        in_specs=[
            pl.BlockSpec((BLK * rpt, LANES), lambda i, blk, ex, lo, hi: (blk[i], 0)),
            pl.BlockSpec((1, d, de), lambda i, blk, ex, lo, hi: (ex[i], 0, 0)),
            pl.BlockSpec((1, d, de), lambda i, blk, ex, lo, hi: (ex[i], 0, 0)),
            pl.BlockSpec((1, de, d), lambda i, blk, ex, lo, hi: (ex[i], 0, 0)),
        ],
        out_specs=pl.BlockSpec((BLK * rpt, LANES), lambda i, blk, ex, lo, hi: (blk[i], 0)),
        scratch_shapes=[pltpu.VMEM((BLK, d), F32), pltpu.VMEM((d, 2 * de), BF16),
                        pltpu.VMEM((de, d), BF16)],
    )
    return pl.pallas_call(
        functools.partial(_experts_kernel, n_items=n_items, rpt=rpt),
        out_shape=jax.ShapeDtypeStruct((a * rpt, LANES), F32),
        grid_spec=grid_spec,
        compiler_params=pltpu.CompilerParams(
            dimension_semantics=("arbitrary",), vmem_limit_bytes=VMEM_LIMIT),
        name="experts",
    )(item_blk, item_exp, item_lo, item_hi, xs, wg, wu, wd)


def _work_items(counts, n_rows):
    n_blocks = n_rows // BLK
    n_items = n_blocks + N_EXPERTS - 1
    counts = counts.astype(jnp.int32)
    ends = jnp.cumsum(counts)
    starts = ends - counts
    first_blk = starts // BLK
    last_blk = jnp.maximum(ends - 1, 0) // BLK
    n_e = jnp.where(counts > 0, last_blk - first_blk + 1, 0)
    item_end = jnp.cumsum(n_e)
    item_start = item_end - n_e
    idx = jnp.arange(n_items, dtype=jnp.int32)
    total = item_end[-1]
    ex = jnp.minimum(jnp.sum((idx[:, None] >= item_end[None, :]).astype(jnp.int32), axis=1),
                     N_EXPERTS - 1)
    onehot = (ex[:, None] == jnp.arange(N_EXPERTS, dtype=jnp.int32)[None, :]).astype(jnp.int32)
    pick = lambda table: jnp.sum(onehot * table[None, :], axis=1)
    blk = pick(first_blk) + (idx - pick(item_start))
    lo = jnp.maximum(pick(starts), blk * BLK) - blk * BLK
    hi = jnp.minimum(pick(ends), (blk + 1) * BLK) - blk * BLK
    valid = idx < total
    last_ex = jnp.max(jnp.where(valid, ex, 0))
    blk = jnp.where(valid, blk, n_blocks - 1).astype(jnp.int32)
    ex = jnp.where(valid, ex, last_ex).astype(jnp.int32)
    lo = jnp.where(valid, lo, 0).astype(jnp.int32)
    hi = jnp.where(valid, hi, 0).astype(jnp.int32)
    return blk, ex, lo, hi


def _combine_kernel(dest_ref, dest_one_ref, dest_ahead_ref, ys_hbm, x1_ref, route_ref, mod_ref,
                    gf_ref, o_ref, ybuf, sems, *, tm, final_norm, rpt):
    i = pl.program_id(0)
    slot = i % N_SLOTS

    def gather(d_ref, slot_idx):
        for t in range(tm):
            for k in range(TOP_K):
                _token_copy(ys_hbm, d_ref[k, t], ybuf.at[slot_idx, k], t, sems.at[slot_idx],
                            rpt).start(priority=k)

    def drain(slot_idx):
        def body(t, carry):
            for k in range(TOP_K):
                _token_copy(ys_hbm, 0, ybuf.at[slot_idx, k], 0, sems.at[slot_idx], rpt).wait()
            return carry
        lax.fori_loop(0, tm, body, 0, unroll=8)

    @pl.when(i == 0)
    def _():
        gather(dest_ref, 0)
        gather(dest_one_ref, 1)

    drain(slot)
    r = route_ref[...]
    y = (r[:, 4:5] * _load_token_rows(ybuf.at[slot, 0], tm, rpt)
         + r[:, 5:6] * _load_token_rows(ybuf.at[slot, 1], tm, rpt))
    mod = mod_ref[0]
    x2 = x1_ref[...] + mod[5:6] * y
    o_ref[...] = _rms(x2, gf_ref[...]) if final_norm else x2

    gather(dest_ahead_ref, (i + GATHER_AHEAD) % N_SLOTS)

    @pl.when(i == pl.num_programs(0) - 1)
    def _():
        for extra in range(1, N_SLOTS):
            drain((i + extra) % N_SLOTS)


def _combine(dest, ys, x1, route, mod, gf, seq, final_norm):
    t, d = x1.shape
    tm = TM_COMB
    n_steps = t // tm
    per_b = seq // tm
    rpt = d // LANES
    return pl.pallas_call(
        functools.partial(_combine_kernel, tm=tm, final_norm=final_norm, rpt=rpt),
        out_shape=jax.ShapeDtypeStruct((t, d), F32),
        grid=(n_steps,),
        in_specs=[
            pl.BlockSpec((SUBLANES, tm), lambda i: (0, i), memory_space=pltpu.SMEM),
            pl.BlockSpec((SUBLANES, tm), lambda i: (0, min(1, n_steps - 1)),
                         memory_space=pltpu.SMEM),
            pl.BlockSpec((SUBLANES, tm), lambda i: (0, jnp.minimum(i + GATHER_AHEAD, n_steps - 1)),
                         memory_space=pltpu.SMEM),
            pl.BlockSpec(memory_space=pl.ANY),
            pl.BlockSpec((tm, d), lambda i: (i, 0)),
            pl.BlockSpec((tm, LANES), lambda i: (i, 0)),
            pl.BlockSpec((1, N_MOD, d), lambda i: (i // per_b, 0, 0)),
            pl.BlockSpec((1, d), lambda i: (0, 0)),
        ],
        out_specs=pl.BlockSpec((tm, d), lambda i: (i, 0)),
        scratch_shapes=[pltpu.VMEM((N_SLOTS, TOP_K, tm * rpt, LANES), F32),
                        pltpu.SemaphoreType.DMA((N_SLOTS,))],
        compiler_params=pltpu.CompilerParams(
            dimension_semantics=("arbitrary",), vmem_limit_bytes=VMEM_LIMIT),
        name="combine",
    )(dest, dest, dest, ys, x1, route, mod, gf)


def _prep_in_weights(w_in, q_rank, kv_rank, conv_dim):
    o1 = q_rank
    o2 = o1 + kv_rank
    o3 = o2 + D_ROPE
    d = w_in.shape[0]
    w_kr = w_in[:, o2:o3]
    half = D_ROPE // 2
    w_kr_rot = jnp.concatenate([-w_kr[:, half:], w_kr[:, :half]], axis=1)
    zl = jnp.zeros((d, D_NOPE), w_in.dtype)
    zr = jnp.zeros((d, LANES - D_NOPE - D_ROPE), w_in.dtype)
    return jnp.concatenate(
        [w_in[:, :o2], zl, w_kr, zr, zl, w_kr_rot, zr, w_in[:, o3:]], axis=1).astype(BF16)


def _prep_q_weights(w_uq):
    r = w_uq.shape[0]
    w = w_uq.reshape(r, HEADS, D_NOPE + D_ROPE)
    nope, rope = w[..., :D_NOPE], w[..., D_NOPE:]
    half = D_ROPE // 2
    rot = jnp.concatenate([-rope[..., half:], rope[..., :half]], axis=-1)
    zr = jnp.zeros((r, HEADS, HEAD_PAD - D_NOPE - D_ROPE), w_uq.dtype)
    w1 = jnp.concatenate([nope, rope, zr], axis=-1).reshape(r, HEADS * HEAD_PAD)
    w2 = jnp.concatenate([jnp.zeros_like(nope), rot, zr], axis=-1).reshape(r, HEADS * HEAD_PAD)
    return w1.astype(BF16), w2.astype(BF16)


def _prep_k_weights(w_uk):
    r = w_uk.shape[0]
    w = w_uk.reshape(r, HEADS, D_NOPE)
    zr = jnp.zeros((r, HEADS, HEAD_PAD - D_NOPE), w_uk.dtype)
    return jnp.concatenate([w, zr], axis=-1).reshape(r, HEADS * HEAD_PAD).astype(BF16)


def _rope_freqs():
    inv_freq = 1.0 / (ROPE_THETA ** (jnp.arange(0, D_ROPE, 2, dtype=F32) / D_ROPE))
    return inv_freq.reshape(D_ROPE // 2, 1)


def kernel(x, c, positions, w_ada, b_ada, norm1_g, w_in, q_norm_g, kv_norm_g, w_uq, w_uk, w_uv,
           conv_w, conv_b, attn_out_g, conv_out_g, w_out, norm2_g, w_router_group, b_router_group,
           w_router_expert, b_router_expert, w_exp_gate, w_exp_up, w_exp_down, final_g):
    bsz, seq, d = x.shape
    depth = w_ada.shape[0]
    q_rank = q_norm_g.shape[-1]
    kv_rank = kv_norm_g.shape[-1]
    conv_dim = conv_w.shape[-1]
    t = bsz * seq
    assert seq % max(TM_PROJ, TQ, TM_OUT, TM_COMB) == 0 and TQ % TK == 0 and TM_PROJ % TK == 0
    assert CHUNK & (CHUNK - 1) == 0 and TQ % CHUNK == 0
    assert t % max(TM_DEST, TM_DISP, TM_COMB) == 0 and (t * TOP_K) % BLK == 0
    assert w_uq.shape[-1] == HEADS * (D_NOPE + D_ROPE) and w_uv.shape[-1] == HEADS * D_V

    pos = positions.reshape(bsz, 1, seq)
    invf = _rope_freqs()
    row = lambda v: v.reshape(1, -1)
    for l in range(depth):
        mod = _adaln(c, w_ada[l], b_ada[l]).reshape(bsz, N_MOD, d)
        win = _prep_in_weights(w_in[l], q_rank, kv_rank, conv_dim)
        wq1, wq2 = _prep_q_weights(w_uq[l])
        wk = _prep_k_weights(w_uk[l])
        wvt = w_uv[l].T.astype(BF16)
        q, k, vt, conv = _inproj(x, mod, pos, row(norm1_g[l]), win, row(q_norm_g[l]),
                                 row(kv_norm_g[l]), wq1, wq2, wk, wvt, invf, conv_w[l],
                                 row(conv_b[l]), row(conv_out_g[l]))
        attn = _attention(q, k, vt)

        wr = jnp.concatenate(
            [w_router_expert[l], w_router_group[l],
             jnp.zeros((d, LANES - N_EXPERTS - N_GROUPS), F32)], axis=1).astype(BF16)
        br = jnp.concatenate(
            [b_router_expert[l], b_router_group[l],
             jnp.zeros((LANES - N_EXPERTS - N_GROUPS,), F32)]).reshape(1, LANES)
        x1, h2, route, rt, cnt = _outproj(x, attn, conv, mod, row(attn_out_g[l]),
                                      w_out[l].astype(BF16), row(norm2_g[l]), wr, br)
        route = route.reshape(t, LANES)
        counts = cnt[0, :N_EXPERTS].astype(jnp.int32)
        starts = jnp.cumsum(counts) - counts
        rpt = d // LANES
        dest = _dest(starts, rt)
        xs = _dispatch(dest, h2, rpt)
        items = _work_items(counts, t * TOP_K)
        ys = _experts(*items, xs, w_exp_gate[l], w_exp_up[l], w_exp_down[l], rpt)
        out = _combine(dest, ys, x1.reshape(t, d), route, mod, row(final_g), seq, l == depth - 1)
        x = out.reshape(bsz, seq, d)
    return x
```
